```python
import jax, jax.numpy as jnp
from jax import lax
import numpy as np

D_MODEL = 4096
BATCH = 2
SEQ = 8192
DEPTH = 1

N_META = 16
MIX_WIDTH = D_MODEL
POOL_WINDOWS = (2, 4, 8, 16)
POOL_WIDTH = D_MODEL // 4
POOL_GROUP = POOL_WIDTH // len(POOL_WINDOWS)
QK_NOPE_DIM = 128
QK_ROPE_DIM = 64
QK_HEAD_DIM = QK_NOPE_DIM + QK_ROPE_DIM
V_HEAD_DIM = 128
MLA_WIDTH = MIX_WIDTH - POOL_WIDTH
MLA_HEADS = MLA_WIDTH // V_HEAD_DIM
Q_LORA_RANK = D_MODEL // 4
KV_LORA_RANK = 512
IN_COLS = POOL_WIDTH + Q_LORA_RANK + KV_LORA_RANK + QK_ROPE_DIM
ROPE_THETA = 10000.0
Q_BLOCK = 128
N_GROUPS = 8
EXPERTS_PER_GROUP = 8
N_EXPERTS = N_GROUPS * EXPERTS_PER_GROUP
TOP_K_INNER = 2
D_EXPERT = 512
MOE_BLOCK = 128
EPS = 1e-6

kernel_name = "hymba_pool_mla_hier_moe_block"


def rms_norm(x, g):
    xf = x.astype(jnp.float32)
    y = xf * lax.rsqrt(jnp.mean(xf * xf, axis=-1, keepdims=True) + EPS)
    return (y * g.astype(jnp.float32)).astype(x.dtype)


def rope_tables(length):
    inv = 1.0 / (ROPE_THETA ** (jnp.arange(0, QK_ROPE_DIM, 2, dtype=jnp.float32) / QK_ROPE_DIM))
    ang = jnp.arange(length, dtype=jnp.float32)[:, None] * inv[None, :]
    return jnp.cos(ang), jnp.sin(ang)


def apply_rope(x, cos, sin):
    xf = x.astype(jnp.float32)
    x1, x2 = jnp.split(xf, 2, axis=-1)
    c = cos[None, :, None, :]
    s = sin[None, :, None, :]
    return jnp.concatenate([x1 * c - x2 * s, x1 * s + x2 * c], axis=-1).astype(x.dtype)


def pool_mixer(u, w_pool, pool_scale):
    L = u.shape[1]
    count = jnp.arange(1, L + 1, dtype=jnp.float32)[None, :, None]
    outs = []
    for gi, w in enumerate(POOL_WINDOWS):
        ug = u[..., gi * POOL_GROUP:(gi + 1) * POOL_GROUP].astype(jnp.float32)
        cs = jnp.cumsum(ug, axis=1)
        lag = jnp.pad(cs, ((0, 0), (w, 0), (0, 0)))[:, :L]
        diff = ((cs - lag) / jnp.minimum(count, float(w)) - ug).astype(u.dtype)
        outs.append(diff @ w_pool[gi])
    return jnp.concatenate(outs, axis=-1) * pool_scale


def attend(qb, k, v, q_pos):
    s = jnp.einsum('bqhd,bkhd->bhqk', qb, k).astype(jnp.float32)
    key_pos = jnp.arange(k.shape[1])
    mask = key_pos[None, :] <= q_pos[:, None]
    s = jnp.where(mask[None, None], s, -jnp.inf)
    p = jax.nn.softmax(s, axis=-1).astype(v.dtype)
    return jnp.einsum('bhqk,bkhd->bqhd', p, v)


def mla_mixer(q_lat, kv_lat, k_rope, q_lat_norm_g, w_uq, kv_lat_norm_g, w_ukv, q_head_norm_g, k_head_norm_g, cos, sin):
    B, L, _ = q_lat.shape
    q = (rms_norm(q_lat, q_lat_norm_g) @ w_uq).reshape(B, L, MLA_HEADS, QK_HEAD_DIM)
    kv = (rms_norm(kv_lat, kv_lat_norm_g) @ w_ukv).reshape(B, L, MLA_HEADS, QK_NOPE_DIM + V_HEAD_DIM)
    k_nope = kv[..., :QK_NOPE_DIM]
    v = kv[..., QK_NOPE_DIM:]
    k_pe = jnp.broadcast_to(k_rope[:, :, None, :], (B, L, MLA_HEADS, QK_ROPE_DIM))
    k = jnp.concatenate([k_nope, k_pe], axis=-1)
    q = rms_norm(q, q_head_norm_g)
    k = rms_norm(k, k_head_norm_g)
    q = jnp.concatenate([q[..., :QK_NOPE_DIM], apply_rope(q[..., QK_NOPE_DIM:], cos, sin)], axis=-1)
    k = jnp.concatenate([k[..., :QK_NOPE_DIM], apply_rope(k[..., QK_NOPE_DIM:], cos, sin)], axis=-1)
    q = q * (QK_HEAD_DIM ** -0.5)
    o_meta = attend(q[:, :N_META], k[:, :N_META], v[:, :N_META], jnp.arange(N_META))
    n_real = L - N_META
    n_blk = n_real // Q_BLOCK
    q_real = q[:, N_META:].reshape(B, n_blk, Q_BLOCK, MLA_HEADS, QK_HEAD_DIM).transpose(1, 0, 2, 3, 4)
    pos = (N_META + jnp.arange(n_real)).reshape(n_blk, Q_BLOCK)
    o_real = lax.map(lambda a: attend(a[0], k, v, a[1]), (q_real, pos))
    o_real = o_real.transpose(1, 0, 2, 3, 4).reshape(B, n_real, MLA_HEADS, V_HEAD_DIM)
    o = jnp.concatenate([o_meta, o_real], axis=1)
    return o.reshape(B, L, MLA_WIDTH)


def hier_moe(h, w_group, b_group, w_expert, b_expert, w_gate, w_up, w_down):
    B, L, D = h.shape
    N = B * L
    xt = h.reshape(N, D)
    xf = xt.astype(jnp.float32)
    group_probs = jax.nn.softmax(xf @ w_group.astype(jnp.float32) + b_group.astype(jnp.float32), axis=-1)
    g_idx = jnp.argmax(group_probs, axis=-1)
    g_p = jnp.take_along_axis(group_probs, g_idx[:, None], axis=-1)
    e_logits = (xf @ w_expert.astype(jnp.float32) + b_expert.astype(jnp.float32)).reshape(N, N_GROUPS, EXPERTS_PER_GROUP)
    in_group = jnp.take_along_axis(e_logits, g_idx[:, None, None], axis=1)[:, 0]
    top_l, top_i = lax.top_k(in_group, TOP_K_INNER)
    gate = g_p * jax.nn.softmax(top_l, axis=-1)
    expert_id = g_idx[:, None] * EXPERTS_PER_GROUP + top_i
    A = N * TOP_K_INNER
    e_flat = expert_id.reshape(A).astype(jnp.int32)
    tok_flat = jnp.repeat(jnp.arange(N, dtype=jnp.int32), TOP_K_INNER)
    w_flat = gate.reshape(A)
    order = jnp.argsort(e_flat)
    e_s = e_flat[order]
    tok_s = tok_flat[order]
    w_s = w_flat[order]
    counts = jnp.bincount(e_flat, length=N_EXPERTS)
    padded = (counts + MOE_BLOCK - 1) // MOE_BLOCK * MOE_BLOCK
    start = jnp.cumsum(counts) - counts
    pend = jnp.cumsum(padded)
    pstart = pend - padded
    dest = pstart[e_s] + jnp.arange(A, dtype=jnp.int32) - start[e_s]
    n_blocks = (A + N_EXPERTS * (MOE_BLOCK - 1) + MOE_BLOCK - 1) // MOE_BLOCK
    P = n_blocks * MOE_BLOCK
    tok_buf = jnp.zeros((P,), jnp.int32).at[dest].set(tok_s)
    w_buf = jnp.zeros((P,), jnp.float32).at[dest].set(w_s)
    blk_start = jnp.arange(n_blocks, dtype=jnp.int32) * MOE_BLOCK
    blk_expert = jnp.minimum(jnp.searchsorted(pend, blk_start, side='right'), N_EXPERTS - 1)

    def run_block(args):
        tok, wt, e = args
        xb = xt[tok]
        hb = jax.nn.silu(xb @ w_gate[e]) * (xb @ w_up[e])
        return (hb @ w_down[e]) * wt[:, None].astype(xb.dtype)

    ys = lax.map(run_block, (tok_buf.reshape(n_blocks, MOE_BLOCK), w_buf.reshape(n_blocks, MOE_BLOCK), blk_expert))
    out = jax.ops.segment_sum(ys.reshape(P, D), tok_buf, num_segments=N)
    return out.reshape(B, L, D)


def setup_inputs(seed: int = 0) -> dict:
    key = jax.random.key(seed)
    ks = jax.random.split(key, 24)

    def nrm(k, shape, fan):
        return jax.random.normal(k, shape, jnp.float32) * (fan ** -0.5)

    def gain(k, shape):
        return 1.0 + 0.02 * jax.random.normal(k, shape, jnp.float32)

    return {
        "x": jax.random.normal(ks[0], (BATCH, SEQ, D_MODEL), jnp.float32),
        "meta_tokens": jax.random.normal(ks[1], (N_META, D_MODEL), jnp.float32),
        "mix_norm_g": gain(ks[2], (DEPTH, D_MODEL)),
        "w_in": nrm(ks[3], (DEPTH, D_MODEL, IN_COLS), D_MODEL),
        "q_lat_norm_g": gain(ks[4], (DEPTH, Q_LORA_RANK)),
        "w_uq": nrm(ks[5], (DEPTH, Q_LORA_RANK, MLA_HEADS * QK_HEAD_DIM), Q_LORA_RANK),
        "kv_lat_norm_g": gain(ks[6], (DEPTH, KV_LORA_RANK)),
        "w_ukv": nrm(ks[7], (DEPTH, KV_LORA_RANK, MLA_HEADS * (QK_NOPE_DIM + V_HEAD_DIM)), KV_LORA_RANK),
        "q_head_norm_g": gain(ks[8], (DEPTH, QK_HEAD_DIM)),
        "k_head_norm_g": gain(ks[9], (DEPTH, QK_HEAD_DIM)),
        "w_pool": nrm(ks[10], (DEPTH, len(POOL_WINDOWS), POOL_GROUP, POOL_GROUP), POOL_GROUP),
        "pool_scale": gain(ks[11], (DEPTH, POOL_WIDTH)),
        "w_out": nrm(ks[12], (DEPTH, MIX_WIDTH, D_MODEL), MIX_WIDTH),
        "ffn_norm_g": gain(ks[13], (DEPTH, D_MODEL)),
        "w_group": nrm(ks[14], (DEPTH, D_MODEL, N_GROUPS), D_MODEL),
        "b_group": 0.01 * jax.random.normal(ks[15], (DEPTH, N_GROUPS), jnp.float32),
        "w_expert": nrm(ks[16], (DEPTH, D_MODEL, N_EXPERTS), D_MODEL),
        "b_expert": 0.01 * jax.random.normal(ks[17], (DEPTH, N_EXPERTS), jnp.float32),
        "w_gate": nrm(ks[18], (DEPTH, N_EXPERTS, D_MODEL, D_EXPERT), D_MODEL),
        "w_up": nrm(ks[19], (DEPTH, N_EXPERTS, D_MODEL, D_EXPERT), D_MODEL),
        "w_down": nrm(ks[20], (DEPTH, N_EXPERTS, D_EXPERT, D_MODEL), D_EXPERT),
    }


def reference(x, meta_tokens, mix_norm_g, w_in, q_lat_norm_g, w_uq, kv_lat_norm_g, w_ukv, q_head_norm_g, k_head_norm_g, w_pool, pool_scale, w_out, ffn_norm_g, w_group, b_group, w_expert, b_expert, w_gate, w_up, w_down):
    B = x.shape[0]
    meta = jnp.broadcast_to(meta_tokens[None].astype(x.dtype), (B, N_META, D_MODEL))
    h = jnp.concatenate([meta, x], axis=1)
    L = h.shape[1]
    cos, sin = rope_tables(L)
    splits = [POOL_WIDTH, POOL_WIDTH + Q_LORA_RANK, POOL_WIDTH + Q_LORA_RANK + KV_LORA_RANK]
    for l in range(DEPTH):
        n = rms_norm(h, mix_norm_g[l])
        proj = n @ w_in[l]
        u, q_lat, kv_lat, k_rope = jnp.split(proj, splits, axis=-1)
        y_pool = pool_mixer(u, w_pool[l], pool_scale[l])
        y_mla = mla_mixer(q_lat, kv_lat, k_rope, q_lat_norm_g[l], w_uq[l], kv_lat_norm_g[l], w_ukv[l],
                          q_head_norm_g[l], k_head_norm_g[l], cos, sin)
        h = h + jnp.concatenate([y_pool, y_mla], axis=-1) @ w_out[l]
        h = h + hier_moe(rms_norm(h, ffn_norm_g[l]), w_group[l], b_group[l], w_expert[l], b_expert[l],
                         w_gate[l], w_up[l], w_down[l])
    return h[:, N_META:]
```

```python
import functools

import jax
import jax.numpy as jnp
from jax import lax
from jax.experimental import pallas as pl
from jax.experimental.pallas import tpu as pltpu

F32 = jnp.float32
BF16 = jnp.bfloat16

N_META = 16
POOL_WINDOWS = (2, 4, 8, 16)
QK_NOPE = 128
QK_ROPE = 64
QK_HEAD = QK_NOPE + QK_ROPE
V_HEAD = 128
HEAD_PAD = 256
KV_LORA = 512
ROPE_THETA = 10000.0
N_GROUPS = 8
EXPERTS_PER_GROUP = 8
N_EXPERTS = N_GROUPS * EXPERTS_PER_GROUP
EPS = 1e-6
ROUTER_ROWS = 128
NEG_BIG = -1e30

VMEM_LIMIT = 56 * 1024 * 1024


def _cparams(sem, vmem=VMEM_LIMIT):
    return pltpu.CompilerParams(dimension_semantics=sem, vmem_limit_bytes=vmem)


def _rms(xf, g, n):
    ss = jnp.sum(xf * xf, axis=-1, keepdims=True)
    return xf * lax.rsqrt(ss * (1.0 / n) + EPS) * g


def _in_proj_kernel(x_ref, meta_ref, g_ref, w_ref, u_ref, lat_ref, mu_ref, mlat_ref, *, n_u, d):
    def proj(xf):
        xn = _rms(xf, g_ref[...], d).astype(BF16)
        return jnp.dot(xn, w_ref[...], preferred_element_type=F32)

    p = proj(x_ref[...])
    u_ref[...] = p[:, :n_u]
    lat_ref[...] = p[:, n_u:]

    @pl.when(pl.program_id(0) == 0)
    def _():
        pm = proj(meta_ref[...])
        mu_ref[...] = pm[:, :n_u]
        mlat_ref[...] = pm[:, n_u:]


def _in_proj(x2d, meta, g, w_p, n_u, tm):
    n, d = x2d.shape
    n_out = w_p.shape[1]
    n_lat = n_out - n_u
    return pl.pallas_call(
        functools.partial(_in_proj_kernel, n_u=n_u, d=d),
        grid=(n // tm,),
        in_specs=[
            pl.BlockSpec((tm, d), lambda i: (i, 0)),
            pl.BlockSpec((N_META, d), lambda i: (0, 0)),
            pl.BlockSpec((1, d), lambda i: (0, 0)),
            pl.BlockSpec((d, n_out), lambda i: (0, 0), pipeline_mode=pl.Buffered(1)),
        ],
        out_specs=[
            pl.BlockSpec((tm, n_u), lambda i: (i, 0)),
            pl.BlockSpec((tm, n_lat), lambda i: (i, 0)),
            pl.BlockSpec((N_META, n_u), lambda i: (0, 0)),
            pl.BlockSpec((N_META, n_lat), lambda i: (0, 0)),
        ],
        out_shape=[
            jax.ShapeDtypeStruct((n, n_u), F32),
            jax.ShapeDtypeStruct((n, n_lat), F32),
            jax.ShapeDtypeStruct((N_META, n_u), F32),
            jax.ShapeDtypeStruct((N_META, n_lat), F32),
        ],
        compiler_params=_cparams(("arbitrary",)),
        name="in_proj",
    )(x2d, meta, g, w_p)


def _qkv_rows(lat, tab, gql_ref, gkvl_ref, gqh_ref, gkh_ref, wuq_ref, wukv_ref,
              store_q, store_k, store_v, *, n_heads, q_rank):
    scale = QK_HEAD ** -0.5
    ql = lat[:, :q_rank]
    kvl = lat[:, q_rank:q_rank + KV_LORA]
    kd = lat[:, q_rank + KV_LORA:q_rank + KV_LORA + 128]
    kr = lat[:, q_rank + KV_LORA + 128:q_rank + KV_LORA + 256]
    tq1 = tab[:, :128]
    tc = tab[:, 128:256]
    ts = tab[:, 256:384]
    qn = _rms(ql, gql_ref[...], q_rank).astype(BF16)
    kvn = _rms(kvl, gkvl_ref[...], KV_LORA).astype(BF16)
    gq0 = gqh_ref[:, :128] * scale
    gq1 = gqh_ref[:, 128:256] * tq1 * scale
    gk0 = gkh_ref[:, :128]
    k_rope = kd * gkh_ref[:, 128:256] * tc + kr * gkh_ref[:, 256:384] * ts
    ss_kr = 0.5 * jnp.sum(kd * kd, axis=-1, keepdims=True)
    inv_n = 1.0 / QK_HEAD
    for h in range(n_heads):
        q = jnp.dot(qn, wuq_ref[:, h * HEAD_PAD:(h + 1) * HEAD_PAD], preferred_element_type=F32)
        x0 = q[:, :128]
        x1 = q[:, 128:]
        ss = jnp.sum(x0 * x0, axis=-1, keepdims=True) + 0.5 * jnp.sum(x1 * x1, axis=-1, keepdims=True)
        rinv = lax.rsqrt(ss * inv_n + EPS)
        store_q(h, jnp.concatenate([x0 * rinv * gq0, x1 * rinv * gq1], axis=-1).astype(BF16))
        kv = jnp.dot(kvn, wukv_ref[:, h * 256:(h + 1) * 256], preferred_element_type=F32)
        kn = kv[:, :128]
        ssk = jnp.sum(kn * kn, axis=-1, keepdims=True) + ss_kr
        rk = lax.rsqrt(ssk * inv_n + EPS)
        store_k(h, jnp.concatenate([kn * rk * gk0, k_rope * rk], axis=-1).astype(BF16))
        store_v(h, kv[:, 128:].astype(BF16))


def _qkv_kernel(lat_ref, tab_ref, mlat_ref, mtab_ref, gql_ref, gkvl_ref, gqh_ref, gkh_ref,
                wuq_ref, wukv_ref, q_ref, k_ref, v_ref, mk_ref, mv_ref, *, n_heads, q_rank):
    def sq(h, val):
        q_ref[h] = val

    def sk(h, val):
        k_ref[h] = val

    def sv(h, val):
        v_ref[h] = val

    _qkv_rows(lat_ref[...], tab_ref[...], gql_ref, gkvl_ref, gqh_ref, gkh_ref, wuq_ref, wukv_ref,
              sq, sk, sv, n_heads=n_heads, q_rank=q_rank)

    @pl.when((pl.program_id(0) == 0) & (pl.program_id(1) == 0))
    def _():
        def smk(h, val):
            mk_ref[h] = val

        def smv(h, val):
            mv_ref[h] = val

        _qkv_rows(mlat_ref[...], mtab_ref[...], gql_ref, gkvl_ref, gqh_ref, gkh_ref, wuq_ref, wukv_ref,
                  lambda h, val: None, smk, smv, n_heads=n_heads, q_rank=q_rank)


def _qkv(lat, tab, mlat, mtab, gql, gkvl, gqh, gkh, wuq_p, wukv, batch, seq, n_heads, q_rank, tm):
    n_lat = lat.shape[1]
    tpb = seq // tm
    const = lambda b, i: (0, 0)
    return pl.pallas_call(
        functools.partial(_qkv_kernel, n_heads=n_heads, q_rank=q_rank),
        grid=(batch, tpb),
        in_specs=[
            pl.BlockSpec((tm, n_lat), lambda b, i: (b * tpb + i, 0)),
            pl.BlockSpec((tm, 384), lambda b, i: (i, 0)),
            pl.BlockSpec((N_META, n_lat), const),
            pl.BlockSpec((N_META, 384), const),
            pl.BlockSpec((1, q_rank), const),
            pl.BlockSpec((1, KV_LORA), const),
            pl.BlockSpec((1, 256), const),
            pl.BlockSpec((1, 384), const),
            pl.BlockSpec(wuq_p.shape, const, pipeline_mode=pl.Buffered(1)),
            pl.BlockSpec(wukv.shape, const, pipeline_mode=pl.Buffered(1)),
        ],
        out_specs=[
            pl.BlockSpec((None, n_heads, tm, HEAD_PAD), lambda b, i: (b, 0, i, 0)),
            pl.BlockSpec((None, n_heads, tm, HEAD_PAD), lambda b, i: (b, 0, i, 0)),
            pl.BlockSpec((None, n_heads, tm, V_HEAD), lambda b, i: (b, 0, i, 0)),
            pl.BlockSpec((n_heads, N_META, HEAD_PAD), lambda b, i: (0, 0, 0)),
            pl.BlockSpec((n_heads, N_META, V_HEAD), lambda b, i: (0, 0, 0)),
        ],
        out_shape=[
            jax.ShapeDtypeStruct((batch, n_heads, seq, HEAD_PAD), BF16),
            jax.ShapeDtypeStruct((batch, n_heads, seq, HEAD_PAD), BF16),
            jax.ShapeDtypeStruct((batch, n_heads, seq, V_HEAD), BF16),
            jax.ShapeDtypeStruct((n_heads, N_META, HEAD_PAD), BF16),
            jax.ShapeDtypeStruct((n_heads, N_META, V_HEAD), BF16),
        ],
        compiler_params=_cparams(("arbitrary", "arbitrary")),
        name="qkv",
    )(lat, tab, mlat, mtab, gql, gkvl, gqh, gkh, wuq_p, wukv)


def _attn_kernel(q_ref, k_ref, v_ref, mk_ref, mv_ref, o_ref, m_ref, l_ref, acc_ref, *, tq):
    qi = pl.program_id(2)
    q = q_ref[...]
    nt = (((1,), (1,)), ((), ()))

    s0 = lax.dot_general(q, mk_ref[...], nt, preferred_element_type=F32)
    m0 = jnp.max(s0, axis=-1, keepdims=True)
    p0 = jnp.exp(s0 - m0)
    m_ref[...] = jnp.broadcast_to(m0, m_ref.shape)
    l_ref[...] = jnp.broadcast_to(jnp.sum(p0, axis=-1, keepdims=True), l_ref.shape)
    acc_ref[...] = jnp.dot(p0.astype(BF16), mv_ref[...], preferred_element_type=F32)

    def step(j, masked):
        start = pl.multiple_of(j * tq, tq)
        kc = k_ref[pl.ds(start, tq), :]
        vc = v_ref[pl.ds(start, tq), :]
        s = lax.dot_general(q, kc, nt, preferred_element_type=F32)
        if masked:
            row = lax.broadcasted_iota(jnp.int32, s.shape, 0)
            col = lax.broadcasted_iota(jnp.int32, s.shape, 1)
            s = jnp.where(col <= row, s, NEG_BIG)
        m_prev = m_ref[:, :1]
        m_new = jnp.maximum(m_prev, jnp.max(s, axis=-1, keepdims=True))
        alpha = jnp.exp(m_prev - m_new)
        p = jnp.exp(s - m_new)
        l_ref[...] = alpha * l_ref[...] + jnp.sum(p, axis=-1, keepdims=True)
        acc_ref[...] = alpha * acc_ref[...] + jnp.dot(p.astype(BF16), vc, preferred_element_type=F32)
        m_ref[...] = jnp.broadcast_to(m_new, m_ref.shape)

    def body(j, c):
        step(j, False)
        return c

    lax.fori_loop(0, qi, body, 0)
    step(qi, True)
    o_ref[...] = (acc_ref[...] / l_ref[...]).astype(o_ref.dtype)


def _attention(q, k, v, mk, mv, tq):
    batch, n_heads, seq, _ = q.shape
    return pl.pallas_call(
        functools.partial(_attn_kernel, tq=tq),
        grid=(batch, n_heads, seq // tq),
        in_specs=[
            pl.BlockSpec((None, None, tq, HEAD_PAD), lambda b, h, i: (b, h, i, 0)),
            pl.BlockSpec((None, None, seq, HEAD_PAD), lambda b, h, i: (b, h, 0, 0)),
            pl.BlockSpec((None, None, seq, V_HEAD), lambda b, h, i: (b, h, 0, 0)),
            pl.BlockSpec((None, N_META, HEAD_PAD), lambda b, h, i: (h, 0, 0)),
            pl.BlockSpec((None, N_META, V_HEAD), lambda b, h, i: (h, 0, 0)),
        ],
        out_specs=pl.BlockSpec((None, tq, V_HEAD), lambda b, h, i: (b, i, h)),
        out_shape=jax.ShapeDtypeStruct((batch, seq, n_heads * V_HEAD), BF16),
        scratch_shapes=[
            pltpu.VMEM((tq, V_HEAD), F32),
            pltpu.VMEM((tq, V_HEAD), F32),
            pltpu.VMEM((tq, V_HEAD), F32),
        ],
        compiler_params=_cparams(("arbitrary", "arbitrary", "arbitrary")),
        name="attention",
    )(q, k, v, mk, mv)


def _out_proj_kernel(u_ref, halo_ref, mu_ref, y_ref, x_ref, wp_ref, ps_ref, wo_ref, o_ref,
                     *, tiles_per_batch, group):
    i = pl.program_id(1)
    first = (i % tiles_per_batch) == 0
    halo = jnp.where(first, mu_ref[...], halo_ref[...])
    ucat = jnp.concatenate([halo, u_ref[...]], axis=0)
    ys = []
    for gi, w in enumerate(POOL_WINDOWS):
        ug = ucat[:, gi * group:(gi + 1) * group]
        s = ug
        shift = 1
        while shift < w:
            s = s + pltpu.roll(s, shift, 0)
            shift *= 2
        diff = (s * (1.0 / w) - ug)[N_META:]
        yg = jnp.dot(diff.astype(BF16), wp_ref[gi], preferred_element_type=F32)
        ys.append((yg * ps_ref[:, gi * group:(gi + 1) * group]).astype(BF16))
    ycat = jnp.concatenate(ys + [y_ref[...]], axis=-1)
    o_ref[...] = x_ref[...] + jnp.dot(ycat, wo_ref[...], preferred_element_type=F32)


def _out_proj(u, mu, y_mla, x2d, w_pool, pool_scale, w_out, seq, tm, tn):
    n, d = x2d.shape
    pool_w = u.shape[1]
    mla_w = y_mla.shape[1]
    group = pool_w // len(POOL_WINDOWS)
    hb = tm // N_META
    return pl.pallas_call(
        functools.partial(_out_proj_kernel, tiles_per_batch=seq // tm, group=group),
        grid=(d // tn, n // tm),
        in_specs=[
            pl.BlockSpec((tm, pool_w), lambda j, i: (i, 0)),
            pl.BlockSpec((N_META, pool_w), lambda j, i: (jnp.maximum(i * hb - 1, 0), 0)),
            pl.BlockSpec((N_META, pool_w), lambda j, i: (0, 0)),
            pl.BlockSpec((tm, mla_w), lambda j, i: (i, 0)),
            pl.BlockSpec((tm, tn), lambda j, i: (i, j)),
            pl.BlockSpec(w_pool.shape, lambda j, i: (0, 0, 0)),
            pl.BlockSpec((1, pool_w), lambda j, i: (0, 0)),
            pl.BlockSpec((d, tn), lambda j, i: (0, j)),
        ],
        out_specs=pl.BlockSpec((tm, tn), lambda j, i: (i, j)),
        out_shape=jax.ShapeDtypeStruct((n, d), F32),
        compiler_params=_cparams(("arbitrary", "arbitrary")),
        name="out_proj",
    )(u, u, mu, y_mla, x2d, w_pool, pool_scale, w_out)


def _pack_bf16_pair(lo, hi):
    lo_b = lax.bitcast_convert_type(lo.astype(BF16).astype(F32), jnp.uint32)
    hi_b = lax.bitcast_convert_type(hi.astype(BF16).astype(F32), jnp.uint32)
    return lax.shift_right_logical(lo_b, jnp.uint32(16)) | hi_b


def _unpack_bf16_pair(w):
    lo = lax.bitcast_convert_type(lax.shift_left(w, jnp.uint32(16)), F32)
    hi = lax.bitcast_convert_type(w & jnp.uint32(0xFFFF0000), F32)
    return lo, hi


def _first_argmax(vals, sub, n):
    mx = jnp.max(vals, axis=0, keepdims=True)
    idx = jnp.min(jnp.where(vals == mx, sub, n), axis=0, keepdims=True)
    return mx, idx


def _router_kernel(h_ref, g_ref, wr_ref, br_ref, xp_ref, ri_ref, rf_ref, cnt_ref, *, d):
    xn = _rms(h_ref[...], g_ref[...], d)
    half = d // 2
    xp_ref[...] = _pack_bf16_pair(xn[:, :half], xn[:, half:])
    logits = lax.dot_general(wr_ref[...], xn, (((1,), (1,)), ((), ())),
                             precision=lax.Precision.HIGHEST, preferred_element_type=F32) + br_ref[...]
    tm = logits.shape[1]
    sub = lax.broadcasted_iota(jnp.int32, (N_GROUPS, tm), 0)
    lg = logits[:N_GROUPS]
    gmax, g_idx = _first_argmax(lg, sub, N_GROUPS)
    g_p = 1.0 / jnp.sum(jnp.exp(lg - gmax), axis=0, keepdims=True)
    in_group = jnp.zeros((EXPERTS_PER_GROUP, tm), F32)
    for g in range(N_GROUPS):
        lo = N_GROUPS + g * EXPERTS_PER_GROUP
        in_group = jnp.where(g_idx == g, logits[lo:lo + EXPERTS_PER_GROUP], in_group)
    m1, i1 = _first_argmax(in_group, sub, EXPERTS_PER_GROUP)
    rest = jnp.where(sub == i1, -jnp.inf, in_group)
    m2, i2 = _first_argmax(rest, sub, EXPERTS_PER_GROUP)
    e = jnp.exp(m2 - m1)
    w1 = g_p / (1.0 + e)
    w2 = g_p * e / (1.0 + e)
    e1 = g_idx * EXPERTS_PER_GROUP + i1
    e2 = g_idx * EXPERTS_PER_GROUP + i2
    ri_ref[...] = jnp.where(sub == 0, e1, jnp.where(sub == 1, e2, 0))
    rf_ref[...] = jnp.where(sub == 0, w1, jnp.where(sub == 1, w2, 0.0))
    sub_e = lax.broadcasted_iota(jnp.int32, (N_EXPERTS, tm), 0)
    hit = jnp.where((sub_e == e1) | (sub_e == e2), 1.0, 0.0)
    cnt = jnp.sum(hit, axis=1, keepdims=True)

    @pl.when(pl.program_id(0) == 0)
    def _():
        cnt_ref[...] = jnp.zeros_like(cnt_ref)

    cnt_ref[...] += jnp.broadcast_to(cnt, cnt_ref.shape)


def _router(h1, g, wr_t, br, tm):
    n, d = h1.shape
    return pl.pallas_call(
        functools.partial(_router_kernel, d=d),
        grid=(n // tm,),
        in_specs=[
            pl.BlockSpec((tm, d), lambda i: (i, 0)),
            pl.BlockSpec((1, d), lambda i: (0, 0)),
            pl.BlockSpec((ROUTER_ROWS, d), lambda i: (0, 0)),
            pl.BlockSpec((ROUTER_ROWS, 1), lambda i: (0, 0)),
        ],
        out_specs=[
            pl.BlockSpec((tm, d // 2), lambda i: (i, 0)),
            pl.BlockSpec((8, tm), lambda i: (0, i)),
            pl.BlockSpec((8, tm), lambda i: (0, i)),
            pl.BlockSpec((N_EXPERTS, 128), lambda i: (0, 0)),
        ],
        out_shape=[
            jax.ShapeDtypeStruct((n, d // 2), jnp.uint32),
            jax.ShapeDtypeStruct((8, n), jnp.int32),
            jax.ShapeDtypeStruct((8, n), F32),
            jax.ShapeDtypeStruct((N_EXPERTS, 128), F32),
        ],
        compiler_params=_cparams(("arbitrary",)),
        name="router",
    )(h1, g, wr_t, br)


def _dest_kernel(ri_ref, pstart_ref, tri_ref, d_ref, carry_ref):
    @pl.when(pl.program_id(0) == 0)
    def _():
        carry_ref[...] = jnp.zeros_like(carry_ref)

    tn = ri_ref.shape[1]
    e1 = ri_ref[0:1, :]
    e2 = ri_ref[1:2, :]
    sub = lax.broadcasted_iota(jnp.int32, (N_EXPERTS, tn), 0)
    oh1 = sub == e1
    oh2 = sub == e2
    oh = jnp.where(oh1 | oh2, 1.0, 0.0)
    c = jnp.dot(oh.astype(BF16), tri_ref[...], preferred_element_type=F32)
    val = c + (pstart_ref[...] + carry_ref[:, :1] - 1.0)
    d1 = jnp.sum(jnp.where(oh1, val, 0.0), axis=0, keepdims=True)
    d2 = jnp.sum(jnp.where(oh2, val, 0.0), axis=0, keepdims=True)
    sub8 = lax.broadcasted_iota(jnp.int32, (8, tn), 0)
    d_ref[...] = jnp.where(sub8 == 0, d1, jnp.where(sub8 == 1, d2, 0.0)).astype(jnp.int32)
    carry_ref[...] += jnp.broadcast_to(jnp.sum(oh, axis=1, keepdims=True), carry_ref.shape)


def _dest(ri, pstart_col, tn):
    n = ri.shape[1]
    tri = (jnp.arange(tn)[:, None] <= jnp.arange(tn)[None, :]).astype(BF16)
    return pl.pallas_call(
        _dest_kernel,
        grid=(n // tn,),
        in_specs=[
            pl.BlockSpec((8, tn), lambda i: (0, i)),
            pl.BlockSpec((N_EXPERTS, 1), lambda i: (0, 0)),
            pl.BlockSpec((tn, tn), lambda i: (0, 0)),
        ],
        out_specs=pl.BlockSpec((8, tn), lambda i: (0, i)),
        out_shape=jax.ShapeDtypeStruct((8, n), jnp.int32),
        scratch_shapes=[pltpu.VMEM((N_EXPERTS, 128), F32)],
        compiler_params=_cparams(("arbitrary",)),
        name="dest",
    )(ri, pstart_col, tri)


def _dispatch_kernel(d1_ref, d2_ref, cnt_ref, pstart_ref, padded_ref, npad_ref, nused_ref,
                     xp_ref, xs_ref, zero_ref, sem, zsem, bsem, *, tm, tb, n_blocks):
    i = pl.program_id(0)
    base = i * tm

    def row_copy(r, dst):
        return pltpu.make_async_copy(xp_ref.at[pl.ds(r, 1), :], xs_ref.at[pl.ds(dst, 1), :], sem)

    def zero_row_copy(dst):
        return pltpu.make_async_copy(zero_ref.at[pl.ds(0, 1), :], xs_ref.at[pl.ds(dst, 1), :], zsem)

    def zero_block_copy(blk):
        start = pl.multiple_of(blk * tb, tb)
        return pltpu.make_async_copy(zero_ref, xs_ref.at[pl.ds(start, tb), :], bsem)

    def issue(r, c):
        row_copy(r, d1_ref[base + r]).start()
        row_copy(r, d2_ref[base + r]).start()
        return c

    lax.fori_loop(0, tm, issue, 0)

    @pl.when(i == 0)
    def _():
        zero_ref[...] = jnp.zeros_like(zero_ref)

        def per_expert(e, c):
            lo = pstart_ref[e] + cnt_ref[e]
            hi = pstart_ref[e] + padded_ref[e]

            def per_row(p, c2):
                zero_row_copy(p).start()
                return c2

            lax.fori_loop(lo, hi, per_row, 0)
            return c

        lax.fori_loop(0, N_EXPERTS, per_expert, 0)

        def per_block(b, c):
            zero_block_copy(b).start()
            return c

        lax.fori_loop(nused_ref[0], n_blocks, per_block, 0)

        def zwait(p, c):
            zero_row_copy(0).wait()
            return c

        lax.fori_loop(0, npad_ref[0], zwait, 0)

        def bwait(b, c):
            zero_block_copy(0).wait()
            return c

        lax.fori_loop(nused_ref[0], n_blocks, bwait, 0)

    def rwait(r, c):
        row_copy(0, 0).wait()
        return c

    lax.fori_loop(0, 2 * tm, rwait, 0)


def _dispatch(d1, d2, cnt, pstart, padded, npad, n_used, xp, n_rows, tm, tb):
    n, half = xp.shape
    grid_spec = pltpu.PrefetchScalarGridSpec(
        num_scalar_prefetch=7,
        grid=(n // tm,),
        in_specs=[pl.BlockSpec((tm, half), lambda i, *_: (i, 0))],
        out_specs=pl.BlockSpec(memory_space=pl.ANY),
        scratch_shapes=[
            pltpu.VMEM((tb, half), jnp.uint32),
            pltpu.SemaphoreType.DMA,
            pltpu.SemaphoreType.DMA,
            pltpu.SemaphoreType.DMA,
        ],
    )
    return pl.pallas_call(
        functools.partial(_dispatch_kernel, tm=tm, tb=tb, n_blocks=n_rows // tb),
        grid_spec=grid_spec,
        out_shape=jax.ShapeDtypeStruct((n_rows, half), jnp.uint32),
        compiler_params=pltpu.CompilerParams(
            dimension_semantics=("arbitrary",), vmem_limit_bytes=VMEM_LIMIT, has_side_effects=True),
        name="dispatch",
    )(d1, d2, cnt, pstart, padded, npad, n_used, xp)


def _experts_kernel(be_ref, nused_ref, xs_ref, wg_ref, wu_ref, wd_ref, ys_ref):
    i = pl.program_id(0)

    @pl.when(i < nused_ref[0])
    def _():
        lo, hi = _unpack_bf16_pair(xs_ref[...])
        lo = lo.astype(BF16)
        hi = hi.astype(BF16)
        half = lo.shape[1]
        g = (jnp.dot(lo, wg_ref[:half], preferred_element_type=F32)
             + jnp.dot(hi, wg_ref[half:], preferred_element_type=F32))
        u = (jnp.dot(lo, wu_ref[:half], preferred_element_type=F32)
             + jnp.dot(hi, wu_ref[half:], preferred_element_type=F32))
        hb = (g / (1.0 + jnp.exp(-g)) * u).astype(BF16)
        y = jnp.dot(hb, wd_ref[...], preferred_element_type=F32)
        ys_ref[...] = _pack_bf16_pair(y[:, :half], y[:, half:])

    @pl.when(i >= nused_ref[0])
    def _():
        ys_ref[...] = jnp.zeros_like(ys_ref)


def _experts(blk_expert, n_used, xs, wg, wu, wd, tm):
    n_rows, half = xs.shape
    _, d, f = wg.shape
    n_blocks = n_rows // tm

    def row_map(i, be, nu):
        return (i, 0)

    def w_map(i, be, nu):
        return (be[i], 0, 0)

    grid_spec = pltpu.PrefetchScalarGridSpec(
        num_scalar_prefetch=2,
        grid=(n_blocks,),
        in_specs=[
            pl.BlockSpec((tm, half), row_map),
            pl.BlockSpec((None, d, f), w_map),
            pl.BlockSpec((None, d, f), w_map),
            pl.BlockSpec((None, f, d), w_map),
        ],
        out_specs=pl.BlockSpec((tm, half), row_map),
    )
    return pl.pallas_call(
        _experts_kernel,
        grid_spec=grid_spec,
        out_shape=jax.ShapeDtypeStruct((n_rows, half), jnp.uint32),
        compiler_params=_cparams(("arbitrary",)),
        name="experts",
    )(blk_expert, n_used, xs, wg, wu, wd)


def _combine_kernel(d1_ref, d2_ref, h_ref, w_ref, ys_ref, o_ref, a_ref, b_ref, sem, *, tm):
    i = pl.program_id(0)
    base = i * tm

    def row_copy(src, r, buf):
        return pltpu.make_async_copy(ys_ref.at[pl.ds(src, 1), :], buf.at[pl.ds(r, 1), :], sem)

    def issue(r, c):
        row_copy(d1_ref[base + r], r, a_ref).start()
        row_copy(d2_ref[base + r], r, b_ref).start()
        return c

    lax.fori_loop(0, tm, issue, 0)

    def rwait(r, c):
        row_copy(0, 0, a_ref).wait()
        return c

    lax.fori_loop(0, 2 * tm, rwait, 0)

    a_lo, a_hi = _unpack_bf16_pair(a_ref[...])
    b_lo, b_hi = _unpack_bf16_pair(b_ref[...])
    w1 = w_ref[:, 0:1]
    w2 = w_ref[:, 1:2]
    half = a_lo.shape[1]
    o_ref[:, :half] = h_ref[:, :half] + (w1 * a_lo + w2 * b_lo)
    o_ref[:, half:] = h_ref[:, half:] + (w1 * a_hi + w2 * b_hi)


def _combine(d1, d2, h1, wcol, ys, tm):
    n, d = h1.shape
    half = d // 2
    grid_spec = pltpu.PrefetchScalarGridSpec(
        num_scalar_prefetch=2,
        grid=(n // tm,),
        in_specs=[
            pl.BlockSpec((tm, d), lambda i, *_: (i, 0)),
            pl.BlockSpec((tm, 8), lambda i, *_: (i, 0)),
            pl.BlockSpec(memory_space=pl.ANY),
        ],
        out_specs=pl.BlockSpec((tm, d), lambda i, *_: (i, 0)),
        scratch_shapes=[
            pltpu.VMEM((tm, half), jnp.uint32),
            pltpu.VMEM((tm, half), jnp.uint32),
            pltpu.SemaphoreType.DMA,
        ],
    )
    return pl.pallas_call(
        functools.partial(_combine_kernel, tm=tm),
        grid_spec=grid_spec,
        out_shape=jax.ShapeDtypeStruct((n, d), F32),
        compiler_params=_cparams(("arbitrary",)),
        name="combine",
    )(d1, d2, h1, wcol, ys)


def _rope_tab(pos):
    inv = 1.0 / (ROPE_THETA ** (jnp.arange(0, QK_ROPE, 2, dtype=F32) / QK_ROPE))
    ang = pos.astype(F32)[:, None] * inv[None, :]
    c = jnp.cos(ang)
    s = jnp.sin(ang)
    cos64 = jnp.concatenate([c, c], axis=-1)
    sin64 = jnp.concatenate([-s, s], axis=-1)
    return jnp.concatenate([cos64, sin64, cos64, cos64, sin64, sin64], axis=-1)


def _swap_halves(a):
    h = a.shape[-1] // 2
    return jnp.concatenate([a[..., h:], a[..., :h]], axis=-1)


def _tile(n, pref):
    t = pref
    while n % t:
        t //= 2
    return t


def kernel(x, meta_tokens, mix_norm_g, w_in, q_lat_norm_g, w_uq, kv_lat_norm_g, w_ukv, q_head_norm_g,
           k_head_norm_g, w_pool, pool_scale, w_out, ffn_norm_g, w_group, b_group, w_expert, b_expert,
           w_gate, w_up, w_down):
    batch, seq, d = x.shape
    depth = w_in.shape[0]
    pool_w = pool_scale.shape[-1]
    q_rank = q_lat_norm_g.shape[-1]
    n_heads = w_uq.shape[-1] // QK_HEAD
    n = batch * seq
    f = w_gate.shape[-1]

    tab = _rope_tab(N_META + jnp.arange(seq))
    mtab = _rope_tab(jnp.arange(N_META))
    tm_exp = _tile(n, 256)
    n_blocks = (2 * n + N_EXPERTS * (tm_exp - 1) + tm_exp - 1) // tm_exp
    n_rows = n_blocks * tm_exp

    h = x.reshape(n, d)
    meta = meta_tokens.astype(F32)
    for l in range(depth):
        k_r = w_in[l][:, pool_w + q_rank + KV_LORA:]
        k_rs = _swap_halves(k_r)
        w_in_p = jnp.concatenate([w_in[l][:, :pool_w + q_rank + KV_LORA], k_r, k_r, k_rs, k_rs],
                                 axis=-1).astype(BF16)
        wq = w_uq[l].reshape(q_rank, n_heads, QK_HEAD)
        wuq_p = jnp.concatenate([wq, _swap_halves(wq[..., QK_NOPE:])], axis=-1)
        wuq_p = wuq_p.reshape(q_rank, n_heads * HEAD_PAD).astype(BF16)
        wukv = w_ukv[l].astype(BF16)
        gq = q_head_norm_g[l]
        gk = k_head_norm_g[l]
        gqh = jnp.concatenate([gq, _swap_halves(gq[QK_NOPE:])])[None, :]
        gk_r = gk[QK_NOPE:]
        gkh = jnp.concatenate([gk[:QK_NOPE], gk_r, gk_r, _swap_halves(gk_r), _swap_halves(gk_r)])[None, :]
        wr_t = jnp.zeros((ROUTER_ROWS, d), F32)
        wr_t = wr_t.at[:N_GROUPS].set(w_group[l].T).at[N_GROUPS:N_GROUPS + N_EXPERTS].set(w_expert[l].T)
        br = jnp.zeros((ROUTER_ROWS, 1), F32)
        br = br.at[:N_GROUPS, 0].set(b_group[l]).at[N_GROUPS:N_GROUPS + N_EXPERTS, 0].set(b_expert[l])

        u, lat, mu, mlat = _in_proj(h, meta, mix_norm_g[l][None, :], w_in_p, pool_w, _tile(n, 256))
        q, k, v, mk, mv = _qkv(lat, tab, mlat, mtab, q_lat_norm_g[l][None, :], kv_lat_norm_g[l][None, :],
                               gqh, gkh, wuq_p, wukv, batch, seq, n_heads, q_rank, _tile(seq, 256))
        y_mla = _attention(q, k, v, mk, mv, _tile(seq, 512)).reshape(n, n_heads * V_HEAD)
        h1 = _out_proj(u, mu, y_mla, h, w_pool[l].astype(BF16), pool_scale[l][None, :],
                       w_out[l].astype(BF16), seq, _tile(seq, 256), _tile(d, 2048))

        xp, ri, rf, cnt = _router(h1, ffn_norm_g[l][None, :], wr_t, br, _tile(n, 512))
        counts = cnt[:, 0].astype(jnp.int32)
        padded = (counts + tm_exp - 1) // tm_exp * tm_exp
        pend = jnp.cumsum(padded)
        pstart = pend - padded
        n_used = (pend[-1] // tm_exp).astype(jnp.int32)
        blk_start = jnp.arange(n_blocks, dtype=jnp.int32) * tm_exp
        blk_expert = jnp.minimum(jnp.searchsorted(pend, blk_start, side="right"), N_EXPERTS - 1)
        blk_expert = blk_expert.astype(jnp.int32)
        npad = (pend[-1] - 2 * n).astype(jnp.int32)

        dest = _dest(ri, pstart.astype(F32)[:, None], _tile(n, 512))
        d1 = dest[0]
        d2 = dest[1]
        xs = _dispatch(d1, d2, counts, pstart.astype(jnp.int32), padded.astype(jnp.int32),
                       npad[None], n_used[None], xp, n_rows, _tile(n, 256), tm_exp)
        ys = _experts(blk_expert, n_used[None], xs, w_gate[l].astype(BF16), w_up[l].astype(BF16),
                      w_down[l].astype(BF16), tm_exp)
        wcol = rf.T
        h = _combine(d1, d2, h1, wcol, ys, _tile(n, 256))
        if l + 1 < depth:
            raise NotImplementedError("meta-token outputs are only produced for a single layer")
    return h.reshape(batch, seq, d)
```

```python
import functools

import jax
import jax.numpy as jnp
from jax import lax
from jax.experimental import pallas as pl
from jax.experimental.pallas import tpu as pltpu

F32 = jnp.float32
BF16 = jnp.bfloat16

N_META = 16
POOL_WINDOWS = (2, 4, 8, 16)
QK_NOPE = 128
QK_ROPE = 64
QK_HEAD = QK_NOPE + QK_ROPE
V_HEAD = 128
HEAD_PAD = 256
KV_LORA = 512
ROPE_THETA = 10000.0
N_GROUPS = 8
EXPERTS_PER_GROUP = 8
N_EXPERTS = N_GROUPS * EXPERTS_PER_GROUP
EPS = 1e-6
ROUTER_ROWS = 128
NEG_BIG = -1e30

VMEM_LIMIT = 56 * 1024 * 1024


def _cparams(sem, vmem=VMEM_LIMIT):
    return pltpu.CompilerParams(dimension_semantics=sem, vmem_limit_bytes=vmem)


def _rms(xf, g, n):
    ss = jnp.sum(xf * xf, axis=-1, keepdims=True)
    return xf * lax.rsqrt(ss * (1.0 / n) + EPS) * g


def _in_proj_kernel(x_ref, meta_ref, g_ref, w_ref, u_ref, lat_ref, mu_ref, mlat_ref, *, n_u, d):
    def proj(xf):
        xn = _rms(xf, g_ref[...], d).astype(BF16)
        return jnp.dot(xn, w_ref[...], preferred_element_type=F32)

    p = proj(x_ref[...])
    u_ref[...] = p[:, :n_u]
    lat_ref[...] = p[:, n_u:]

    @pl.when(pl.program_id(0) == 0)
    def _():
        pm = proj(meta_ref[...])
        mu_ref[...] = pm[:, :n_u]
        mlat_ref[...] = pm[:, n_u:]


def _in_proj(x2d, meta, g, w_p, n_u, tm):
    n, d = x2d.shape
    n_out = w_p.shape[1]
    n_lat = n_out - n_u
    return pl.pallas_call(
        functools.partial(_in_proj_kernel, n_u=n_u, d=d),
        grid=(n // tm,),
        in_specs=[
            pl.BlockSpec((tm, d), lambda i: (i, 0)),
            pl.BlockSpec((N_META, d), lambda i: (0, 0)),
            pl.BlockSpec((1, d), lambda i: (0, 0)),
            pl.BlockSpec((d, n_out), lambda i: (0, 0), pipeline_mode=pl.Buffered(1)),
        ],
        out_specs=[
            pl.BlockSpec((tm, n_u), lambda i: (i, 0)),
            pl.BlockSpec((tm, n_lat), lambda i: (i, 0)),
            pl.BlockSpec((N_META, n_u), lambda i: (0, 0)),
            pl.BlockSpec((N_META, n_lat), lambda i: (0, 0)),
        ],
        out_shape=[
            jax.ShapeDtypeStruct((n, n_u), F32),
            jax.ShapeDtypeStruct((n, n_lat), F32),
            jax.ShapeDtypeStruct((N_META, n_u), F32),
            jax.ShapeDtypeStruct((N_META, n_lat), F32),
        ],
        compiler_params=_cparams(("arbitrary",)),
        name="in_proj",
    )(x2d, meta, g, w_p)


def _lanes(a, m):
    return jnp.concatenate([a] * (m // 128), axis=1) if m > 128 else a


def _kv_rows(lat, tab, gkvl_ref, gkh_ref, wk_ref, wvt_ref, store_k, store_vt, *, n_heads, q_rank):
    kvl = lat[:, q_rank:q_rank + KV_LORA]
    kd = lat[:, q_rank + KV_LORA:q_rank + KV_LORA + 128]
    kr = lat[:, q_rank + KV_LORA + 128:q_rank + KV_LORA + 256]
    kvn = _rms(kvl, gkvl_ref[...], KV_LORA)
    kvn_b = kvn.astype(BF16)
    kvn_t = kvn.T.astype(BF16)
    gk0 = gkh_ref[:, :128]
    k_rope = kd * gkh_ref[:, 128:256] * tab[:, :128] + kr * gkh_ref[:, 256:384] * tab[:, 128:256]
    ss_kr = 0.5 * jnp.sum(kd * kd, axis=-1, keepdims=True)
    inv_n = 1.0 / QK_HEAD
    for p in range(n_heads // 2):
        kk = jnp.dot(kvn_b, wk_ref[:, p * 256:(p + 1) * 256], preferred_element_type=F32)
        vv = jnp.dot(wvt_ref[p * 256:(p + 1) * 256, :], kvn_t, preferred_element_type=F32)
        for t in range(2):
            kn = kk[:, t * 128:(t + 1) * 128]
            ssk = jnp.sum(kn * kn, axis=-1, keepdims=True) + ss_kr
            rk = lax.rsqrt(ssk * inv_n + EPS)
            store_k(2 * p + t, jnp.concatenate([kn * rk * gk0, k_rope * rk], axis=-1).astype(BF16))
            store_vt(2 * p + t, vv[t * 128:(t + 1) * 128].astype(BF16))


def _qkv_kernel(lat_ref, tab_ref, tqt_ref, mlat_ref, mtab_ref, gql_ref, gkvl_ref, gqc_ref, gkh_ref,
                wqt_ref, wk_ref, wvt_ref, qt_ref, k_ref, vt_ref, mk_ref, mvt_ref, *, n_heads, q_rank):
    lat = lat_ref[...]
    m = lat.shape[0]

    def sk(h, val):
        k_ref[h] = val

    def svt(h, val):
        vt_ref[h] = val

    _kv_rows(lat, tab_ref[...], gkvl_ref, gkh_ref, wk_ref, wvt_ref, sk, svt, n_heads=n_heads, q_rank=q_rank)

    scale = QK_HEAD ** -0.5
    qn_t = _rms(lat[:, :q_rank], gql_ref[...], q_rank).T.astype(BF16)
    gq0 = _lanes(gqc_ref[:128, :], m) * scale
    gq1 = _lanes(gqc_ref[128:, :], m) * tqt_ref[...] * scale
    inv_n = 1.0 / QK_HEAD
    for h in range(n_heads):
        xh = jnp.dot(wqt_ref[h * HEAD_PAD:(h + 1) * HEAD_PAD, :], qn_t, preferred_element_type=F32)
        x0 = xh[:128]
        x1 = xh[128:]
        ss = jnp.sum(x0 * x0, axis=0, keepdims=True) + 0.5 * jnp.sum(x1 * x1, axis=0, keepdims=True)
        rinv = lax.rsqrt(ss * inv_n + EPS)
        qt_ref[h] = jnp.concatenate([x0 * rinv * gq0, x1 * rinv * gq1], axis=0).astype(BF16)

    @pl.when((pl.program_id(0) == 0) & (pl.program_id(1) == 0))
    def _():
        def smk(h, val):
            mk_ref[h] = val

        def smvt(h, val):
            mvt_ref[h] = val

        _kv_rows(mlat_ref[...], mtab_ref[...], gkvl_ref, gkh_ref, wk_ref, wvt_ref, smk, smvt,
                 n_heads=n_heads, q_rank=q_rank)


def _qkv(lat, tab, tqt, mlat, mtab, gql, gkvl, gqc, gkh, wqt, wk, wvt, batch, seq, n_heads, q_rank, tm):
    n_lat = lat.shape[1]
    tpb = seq // tm
    mp = mlat.shape[0]
    const = lambda b, i: (0, 0)
    return pl.pallas_call(
        functools.partial(_qkv_kernel, n_heads=n_heads, q_rank=q_rank),
        grid=(batch, tpb),
        in_specs=[
            pl.BlockSpec((tm, n_lat), lambda b, i: (b * tpb + i, 0)),
            pl.BlockSpec((tm, 256), lambda b, i: (i, 0)),
            pl.BlockSpec((128, tm), lambda b, i: (0, i)),
            pl.BlockSpec((mp, n_lat), const),
            pl.BlockSpec((mp, 256), const),
            pl.BlockSpec((1, q_rank), const),
            pl.BlockSpec((1, KV_LORA), const),
            pl.BlockSpec((256, 128), const),
            pl.BlockSpec((1, 384), const),
            pl.BlockSpec(wqt.shape, const, pipeline_mode=pl.Buffered(1)),
            pl.BlockSpec(wk.shape, const, pipeline_mode=pl.Buffered(1)),
            pl.BlockSpec(wvt.shape, const, pipeline_mode=pl.Buffered(1)),
        ],
        out_specs=[
            pl.BlockSpec((None, n_heads, HEAD_PAD, tm), lambda b, i: (b, 0, 0, i)),
            pl.BlockSpec((None, n_heads, tm, HEAD_PAD), lambda b, i: (b, 0, i, 0)),
            pl.BlockSpec((None, n_heads, None, V_HEAD, tm), lambda b, i: (b, 0, i, 0, 0)),
            pl.BlockSpec((n_heads, mp, HEAD_PAD), lambda b, i: (0, 0, 0)),
            pl.BlockSpec((n_heads, V_HEAD, mp), lambda b, i: (0, 0, 0)),
        ],
        out_shape=[
            jax.ShapeDtypeStruct((batch, n_heads, HEAD_PAD, seq), BF16),
            jax.ShapeDtypeStruct((batch, n_heads, seq, HEAD_PAD), BF16),
            jax.ShapeDtypeStruct((batch, n_heads, tpb, V_HEAD, tm), BF16),
            jax.ShapeDtypeStruct((n_heads, mp, HEAD_PAD), BF16),
            jax.ShapeDtypeStruct((n_heads, V_HEAD, mp), BF16),
        ],
        compiler_params=_cparams(("arbitrary", "arbitrary")),
        name="qkv",
    )(lat, tab, tqt, mlat, mtab, gql, gkvl, gqc, gkh, wqt, wk, wvt)


ATTN_HEADS = 2
ATTN_STRIP = 256


def _attn_kernel(qt_ref, k_ref, vt_ref, mk_ref, mvt_ref, o_ref, acc_ref, *, tq, tv):
    qi = pl.program_id(2)
    nh = qt_ref.shape[0]
    sw = ATTN_STRIP
    chains = [(hh, x) for x in range(tq // sw) for hh in range(nh)]
    qts = [qt_ref[hh, :, x * sw:(x + 1) * sw] for hh, x in chains]

    carry = []
    for ci, (hh, x) in enumerate(chains):
        s0 = jnp.dot(mk_ref[hh], qts[ci], preferred_element_type=F32)
        s0 = jnp.where(lax.broadcasted_iota(jnp.int32, s0.shape, 0) < N_META, s0, NEG_BIG)
        m0 = jnp.max(s0, axis=0, keepdims=True)
        p0 = jnp.exp(s0 - m0)
        carry += [m0, jnp.sum(p0, axis=0, keepdims=True)]
        acc_ref[hh, :, x * sw:(x + 1) * sw] = jnp.dot(mvt_ref[hh], p0.astype(BF16),
                                                     preferred_element_type=F32)

    def scores(ci, j, masked):
        hh, x = chains[ci]
        start = pl.multiple_of(j * tq, tq)
        s = jnp.dot(k_ref[hh, pl.ds(start, tq), :], qts[ci], preferred_element_type=F32)
        if masked:
            key = lax.broadcasted_iota(jnp.int32, s.shape, 0)
            qry = lax.broadcasted_iota(jnp.int32, s.shape, 1) + x * sw
            s = jnp.where(key <= qry, s, NEG_BIG)
        return s

    def softmax(s, m_prev, l_prev):
        m_new = jnp.maximum(m_prev, jnp.max(s, axis=0, keepdims=True))
        alpha = jnp.exp(m_prev - m_new)
        p = jnp.exp(s - m_new)
        l_new = alpha * l_prev + jnp.sum(p, axis=0, keepdims=True)
        return m_new, l_new, alpha, p.astype(BF16)

    def values(ci, j, alpha, pb):
        hh, x = chains[ci]
        pv = jnp.dot(vt_ref[hh, j * (tq // tv)], pb[:tv], preferred_element_type=F32)
        for c in range(1, tq // tv):
            pv += jnp.dot(vt_ref[hh, j * (tq // tv) + c], pb[c * tv:(c + 1) * tv],
                          preferred_element_type=F32)
        acc_ref[hh, :, x * sw:(x + 1) * sw] = alpha * acc_ref[hh, :, x * sw:(x + 1) * sw] + pv

    def step(j, c, masked):
        n = len(chains)
        out = [None] * (2 * n)
        s_vals = {}
        sm_vals = {}
        for t in range(n + 2):
            if t < n:
                s_vals[t] = scores(t, j, masked)
            if 0 <= t - 1 < n:
                ci = t - 1
                m_new, l_new, alpha, pb = softmax(s_vals.pop(ci), c[2 * ci], c[2 * ci + 1])
                out[2 * ci] = m_new
                out[2 * ci + 1] = l_new
                sm_vals[ci] = (alpha, pb)
            if 0 <= t - 2 < n:
                ci = t - 2
                values(ci, j, *sm_vals.pop(ci))
        return tuple(out)

    c = lax.fori_loop(0, qi, lambda j, c: step(j, c, False), tuple(carry))
    c = step(qi, c, True)
    for ci, (hh, x) in enumerate(chains):
        o_ref[x * sw:(x + 1) * sw, hh * V_HEAD:(hh + 1) * V_HEAD] = (
            acc_ref[hh, :, x * sw:(x + 1) * sw] / c[2 * ci + 1]).T.astype(o_ref.dtype)


def _attention(qt, k, vt, mk, mvt, tq):
    batch, n_heads, seq, _ = k.shape
    n_chunks, tv = vt.shape[2], vt.shape[4]
    mp = mk.shape[1]
    nh = ATTN_HEADS
    return pl.pallas_call(
        functools.partial(_attn_kernel, tq=tq, tv=tv),
        grid=(batch, n_heads // nh, seq // tq),
        in_specs=[
            pl.BlockSpec((None, nh, HEAD_PAD, tq), lambda b, h, i: (b, h, 0, i)),
            pl.BlockSpec((None, nh, seq, HEAD_PAD), lambda b, h, i: (b, h, 0, 0)),
            pl.BlockSpec((None, nh, n_chunks, V_HEAD, tv), lambda b, h, i: (b, h, 0, 0, 0)),
            pl.BlockSpec((nh, mp, HEAD_PAD), lambda b, h, i: (h, 0, 0)),
            pl.BlockSpec((nh, V_HEAD, mp), lambda b, h, i: (h, 0, 0)),
        ],
        out_specs=pl.BlockSpec((None, tq, nh * V_HEAD), lambda b, h, i: (b, i, h)),
        out_shape=jax.ShapeDtypeStruct((batch, seq, n_heads * V_HEAD), BF16),
        scratch_shapes=[pltpu.VMEM((nh, V_HEAD, tq), F32)],
        compiler_params=_cparams(("arbitrary", "arbitrary", "arbitrary")),
        name="attention",
    )(qt, k, vt, mk, mvt)


def _out_proj_kernel(u_ref, halo_ref, mu_ref, y_ref, x_ref, wp_ref, ps_ref, wo_ref, o_ref,
                     *, tiles_per_batch, group):
    i = pl.program_id(1)
    first = (i % tiles_per_batch) == 0
    halo = jnp.where(first, mu_ref[...], halo_ref[...])
    ucat = jnp.concatenate([halo, u_ref[...]], axis=0)
    ys = []
    for gi, w in enumerate(POOL_WINDOWS):
        ug = ucat[:, gi * group:(gi + 1) * group]
        s = ug
        shift = 1
        while shift < w:
            s = s + pltpu.roll(s, shift, 0)
            shift *= 2
        diff = (s * (1.0 / w) - ug)[N_META:]
        yg = jnp.dot(diff.astype(BF16), wp_ref[gi], preferred_element_type=F32)
        ys.append((yg * ps_ref[:, gi * group:(gi + 1) * group]).astype(BF16))
    ycat = jnp.concatenate(ys + [y_ref[...]], axis=-1)
    o_ref[...] = x_ref[...] + jnp.dot(ycat, wo_ref[...], preferred_element_type=F32)


def _out_proj(u, mu, y_mla, x2d, w_pool, pool_scale, w_out, seq, tm, tn):
    n, d = x2d.shape
    pool_w = u.shape[1]
    mla_w = y_mla.shape[1]
    group = pool_w // len(POOL_WINDOWS)
    hb = tm // N_META
    return pl.pallas_call(
        functools.partial(_out_proj_kernel, tiles_per_batch=seq // tm, group=group),
        grid=(d // tn, n // tm),
        in_specs=[
            pl.BlockSpec((tm, pool_w), lambda j, i: (i, 0)),
            pl.BlockSpec((N_META, pool_w), lambda j, i: (jnp.maximum(i * hb - 1, 0), 0)),
            pl.BlockSpec((N_META, pool_w), lambda j, i: (0, 0)),
            pl.BlockSpec((tm, mla_w), lambda j, i: (i, 0)),
            pl.BlockSpec((tm, tn), lambda j, i: (i, j)),
            pl.BlockSpec(w_pool.shape, lambda j, i: (0, 0, 0)),
            pl.BlockSpec((1, pool_w), lambda j, i: (0, 0)),
            pl.BlockSpec((d, tn), lambda j, i: (0, j)),
        ],
        out_specs=pl.BlockSpec((tm, tn), lambda j, i: (i, j)),
        out_shape=jax.ShapeDtypeStruct((n, d), F32),
        compiler_params=_cparams(("arbitrary", "arbitrary")),
        name="out_proj",
    )(u, u, mu, y_mla, x2d, w_pool, pool_scale, w_out)


def _pack_bf16_pair(lo, hi):
    lo_b = lax.bitcast_convert_type(lo.astype(BF16).astype(F32), jnp.uint32)
    hi_b = lax.bitcast_convert_type(hi.astype(BF16).astype(F32), jnp.uint32)
    return lax.shift_right_logical(lo_b, jnp.uint32(16)) | hi_b


def _unpack_bf16_pair(w):
    lo = lax.bitcast_convert_type(lax.shift_left(w, jnp.uint32(16)), F32)
    hi = lax.bitcast_convert_type(w & jnp.uint32(0xFFFF0000), F32)
    return lo, hi


def _first_argmax(vals, sub, n):
    mx = jnp.max(vals, axis=0, keepdims=True)
    idx = jnp.min(jnp.where(vals == mx, sub, n), axis=0, keepdims=True)
    return mx, idx


def _router_kernel(h_ref, g_ref, wr_ref, br_ref, xp_ref, ri_ref, rf_ref, cnt_ref, *, d):
    xn = _rms(h_ref[...], g_ref[...], d)
    half = d // 2
    xp_ref[...] = _pack_bf16_pair(xn[:, :half], xn[:, half:])
    logits = lax.dot_general(wr_ref[...], xn, (((1,), (1,)), ((), ())),
                             precision=lax.Precision.HIGHEST, preferred_element_type=F32) + br_ref[...]
    tm = logits.shape[1]
    sub = lax.broadcasted_iota(jnp.int32, (N_GROUPS, tm), 0)
    lg = logits[:N_GROUPS]
    gmax, g_idx = _first_argmax(lg, sub, N_GROUPS)
    g_p = 1.0 / jnp.sum(jnp.exp(lg - gmax), axis=0, keepdims=True)
    in_group = jnp.zeros((EXPERTS_PER_GROUP, tm), F32)
    for g in range(N_GROUPS):
        lo = N_GROUPS + g * EXPERTS_PER_GROUP
        in_group = jnp.where(g_idx == g, logits[lo:lo + EXPERTS_PER_GROUP], in_group)
    m1, i1 = _first_argmax(in_group, sub, EXPERTS_PER_GROUP)
    rest = jnp.where(sub == i1, -jnp.inf, in_group)
    m2, i2 = _first_argmax(rest, sub, EXPERTS_PER_GROUP)
    e = jnp.exp(m2 - m1)
    w1 = g_p / (1.0 + e)
    w2 = g_p * e / (1.0 + e)
    e1 = g_idx * EXPERTS_PER_GROUP + i1
    e2 = g_idx * EXPERTS_PER_GROUP + i2
    ri_ref[...] = jnp.where(sub == 0, e1, jnp.where(sub == 1, e2, 0))
    rf_ref[...] = jnp.where(sub == 0, w1, jnp.where(sub == 1, w2, 0.0))
    sub_e = lax.broadcasted_iota(jnp.int32, (N_EXPERTS, tm), 0)
    hit = jnp.where((sub_e == e1) | (sub_e == e2), 1.0, 0.0)
    cnt = jnp.sum(hit, axis=1, keepdims=True)

    @pl.when(pl.program_id(0) == 0)
    def _():
        cnt_ref[...] = jnp.zeros_like(cnt_ref)

    cnt_ref[...] += jnp.broadcast_to(cnt, cnt_ref.shape)


def _router(h1, g, wr_t, br, tm):
    n, d = h1.shape
    return pl.pallas_call(
        functools.partial(_router_kernel, d=d),
        grid=(n // tm,),
        in_specs=[
            pl.BlockSpec((tm, d), lambda i: (i, 0)),
            pl.BlockSpec((1, d), lambda i: (0, 0)),
            pl.BlockSpec((ROUTER_ROWS, d), lambda i: (0, 0)),
            pl.BlockSpec((ROUTER_ROWS, 1), lambda i: (0, 0)),
        ],
        out_specs=[
            pl.BlockSpec((tm, d // 2), lambda i: (i, 0)),
            pl.BlockSpec((8, tm), lambda i: (0, i)),
            pl.BlockSpec((8, tm), lambda i: (0, i)),
            pl.BlockSpec((N_EXPERTS, 128), lambda i: (0, 0)),
        ],
        out_shape=[
            jax.ShapeDtypeStruct((n, d // 2), jnp.uint32),
            jax.ShapeDtypeStruct((8, n), jnp.int32),
            jax.ShapeDtypeStruct((8, n), F32),
            jax.ShapeDtypeStruct((N_EXPERTS, 128), F32),
        ],
        compiler_params=_cparams(("arbitrary",)),
        name="router",
    )(h1, g, wr_t, br)


def _dest_kernel(ri_ref, pstart_ref, tri_ref, d_ref, carry_ref):
    @pl.when(pl.program_id(0) == 0)
    def _():
        carry_ref[...] = jnp.zeros_like(carry_ref)

    tn = ri_ref.shape[1]
    e1 = ri_ref[0:1, :]
    e2 = ri_ref[1:2, :]
    sub = lax.broadcasted_iota(jnp.int32, (N_EXPERTS, tn), 0)
    oh1 = sub == e1
    oh2 = sub == e2
    oh = jnp.where(oh1 | oh2, 1.0, 0.0)
    c = jnp.dot(oh.astype(BF16), tri_ref[...], preferred_element_type=F32)
    val = c + (pstart_ref[...] + carry_ref[:, :1] - 1.0)
    d1 = jnp.sum(jnp.where(oh1, val, 0.0), axis=0, keepdims=True)
    d2 = jnp.sum(jnp.where(oh2, val, 0.0), axis=0, keepdims=True)
    sub8 = lax.broadcasted_iota(jnp.int32, (8, tn), 0)
    d_ref[...] = jnp.where(sub8 == 0, d1, jnp.where(sub8 == 1, d2, 0.0)).astype(jnp.int32)
    carry_ref[...] += jnp.broadcast_to(jnp.sum(oh, axis=1, keepdims=True), carry_ref.shape)


def _dest(ri, pstart_col, tn):
    n = ri.shape[1]
    tri = (jnp.arange(tn)[:, None] <= jnp.arange(tn)[None, :]).astype(BF16)
    return pl.pallas_call(
        _dest_kernel,
        grid=(n // tn,),
        in_specs=[
            pl.BlockSpec((8, tn), lambda i: (0, i)),
            pl.BlockSpec((N_EXPERTS, 1), lambda i: (0, 0)),
            pl.BlockSpec((tn, tn), lambda i: (0, 0)),
        ],
        out_specs=pl.BlockSpec((8, tn), lambda i: (0, i)),
        out_shape=jax.ShapeDtypeStruct((8, n), jnp.int32),
        scratch_shapes=[pltpu.VMEM((N_EXPERTS, 128), F32)],
        compiler_params=_cparams(("arbitrary",)),
        name="dest",
    )(ri, pstart_col, tri)


def _dispatch_kernel(d1_ref, d2_ref, cnt_ref, pstart_ref, padded_ref, npad_ref, nused_ref,
                     xp_ref, xs_ref, zero_ref, sem, zsem, bsem, *, tm, tb, n_blocks):
    i = pl.program_id(0)
    base = i * tm

    def row_copy(r, dst):
        return pltpu.make_async_copy(xp_ref.at[pl.ds(r, 1), :], xs_ref.at[pl.ds(dst, 1), :], sem)

    def zero_row_copy(dst):
        return pltpu.make_async_copy(zero_ref.at[pl.ds(0, 1), :], xs_ref.at[pl.ds(dst, 1), :], zsem)

    def zero_block_copy(blk):
        start = pl.multiple_of(blk * tb, tb)
        return pltpu.make_async_copy(zero_ref, xs_ref.at[pl.ds(start, tb), :], bsem)

    def issue(r, c):
        row_copy(r, d1_ref[base + r]).start()
        row_copy(r, d2_ref[base + r]).start()
        return c

    lax.fori_loop(0, tm, issue, 0)

    @pl.when(i == 0)
    def _():
        zero_ref[...] = jnp.zeros_like(zero_ref)

        def per_expert(e, c):
            lo = pstart_ref[e] + cnt_ref[e]
            hi = pstart_ref[e] + padded_ref[e]

            def per_row(p, c2):
                zero_row_copy(p).start()
                return c2

            lax.fori_loop(lo, hi, per_row, 0)
            return c

        lax.fori_loop(0, N_EXPERTS, per_expert, 0)

        def per_block(b, c):
            zero_block_copy(b).start()
            return c

        lax.fori_loop(nused_ref[0], n_blocks, per_block, 0)

        def zwait(p, c):
            zero_row_copy(0).wait()
            return c

        lax.fori_loop(0, npad_ref[0], zwait, 0)

        def bwait(b, c):
            zero_block_copy(0).wait()
            return c

        lax.fori_loop(nused_ref[0], n_blocks, bwait, 0)

    def rwait(r, c):
        row_copy(0, 0).wait()
        return c

    lax.fori_loop(0, 2 * tm, rwait, 0)


def _dispatch(d1, d2, cnt, pstart, padded, npad, n_used, xp, n_rows, tm, tb):
    n, half = xp.shape
    grid_spec = pltpu.PrefetchScalarGridSpec(
        num_scalar_prefetch=7,
        grid=(n // tm,),
        in_specs=[pl.BlockSpec((tm, half), lambda i, *_: (i, 0))],
        out_specs=pl.BlockSpec(memory_space=pl.ANY),
        scratch_shapes=[
            pltpu.VMEM((tb, half), jnp.uint32),
            pltpu.SemaphoreType.DMA,
            pltpu.SemaphoreType.DMA,
            pltpu.SemaphoreType.DMA,
        ],
    )
    return pl.pallas_call(
        functools.partial(_dispatch_kernel, tm=tm, tb=tb, n_blocks=n_rows // tb),
        grid_spec=grid_spec,
        out_shape=jax.ShapeDtypeStruct((n_rows, half), jnp.uint32),
        compiler_params=pltpu.CompilerParams(
            dimension_semantics=("arbitrary",), vmem_limit_bytes=VMEM_LIMIT, has_side_effects=True),
        name="dispatch",
    )(d1, d2, cnt, pstart, padded, npad, n_used, xp)


def _experts_kernel(be_ref, nused_ref, xs_ref, wg_ref, wu_ref, wd_ref, ys_ref):
    i = pl.program_id(0)

    @pl.when(i < nused_ref[0])
    def _():
        lo, hi = _unpack_bf16_pair(xs_ref[...])
        lo = lo.astype(BF16)
        hi = hi.astype(BF16)
        half = lo.shape[1]
        g = (jnp.dot(lo, wg_ref[:half], preferred_element_type=F32)
             + jnp.dot(hi, wg_ref[half:], preferred_element_type=F32))
        u = (jnp.dot(lo, wu_ref[:half], preferred_element_type=F32)
             + jnp.dot(hi, wu_ref[half:], preferred_element_type=F32))
        hb = (g / (1.0 + jnp.exp(-g)) * u).astype(BF16)
        y = jnp.dot(hb, wd_ref[...], preferred_element_type=F32)
        ys_ref[...] = _pack_bf16_pair(y[:, :half], y[:, half:])

    @pl.when(i >= nused_ref[0])
    def _():
        ys_ref[...] = jnp.zeros_like(ys_ref)


def _experts(blk_expert, n_used, xs, wg, wu, wd, tm):
    n_rows, half = xs.shape
    _, d, f = wg.shape
    n_blocks = n_rows // tm

    def row_map(i, be, nu):
        return (i, 0)

    def w_map(i, be, nu):
        return (be[i], 0, 0)

    grid_spec = pltpu.PrefetchScalarGridSpec(
        num_scalar_prefetch=2,
        grid=(n_blocks,),
        in_specs=[
            pl.BlockSpec((tm, half), row_map),
            pl.BlockSpec((None, d, f), w_map),
            pl.BlockSpec((None, d, f), w_map),
            pl.BlockSpec((None, f, d), w_map),
        ],
        out_specs=pl.BlockSpec((tm, half), row_map),
    )
    return pl.pallas_call(
        _experts_kernel,
        grid_spec=grid_spec,
        out_shape=jax.ShapeDtypeStruct((n_rows, half), jnp.uint32),
        compiler_params=_cparams(("arbitrary",)),
        name="experts",
    )(blk_expert, n_used, xs, wg, wu, wd)


def _combine_kernel(d1_ref, d2_ref, h_ref, w_ref, ys_ref, o_ref, a_ref, b_ref, sem, *, tm):
    i = pl.program_id(0)
    base = i * tm

    def row_copy(src, r, buf):
        return pltpu.make_async_copy(ys_ref.at[pl.ds(src, 1), :], buf.at[pl.ds(r, 1), :], sem)

    def issue(r, c):
        row_copy(d1_ref[base + r], r, a_ref).start()
        row_copy(d2_ref[base + r], r, b_ref).start()
        return c

    lax.fori_loop(0, tm, issue, 0)

    def rwait(r, c):
        row_copy(0, 0, a_ref).wait()
        return c

    lax.fori_loop(0, 2 * tm, rwait, 0)

    a_lo, a_hi = _unpack_bf16_pair(a_ref[...])
    b_lo, b_hi = _unpack_bf16_pair(b_ref[...])
    w1 = w_ref[:, 0:1]
    w2 = w_ref[:, 1:2]
    half = a_lo.shape[1]
    o_ref[:, :half] = h_ref[:, :half] + (w1 * a_lo + w2 * b_lo)
    o_ref[:, half:] = h_ref[:, half:] + (w1 * a_hi + w2 * b_hi)


def _combine(d1, d2, h1, wcol, ys, tm):
    n, d = h1.shape
    half = d // 2
    grid_spec = pltpu.PrefetchScalarGridSpec(
        num_scalar_prefetch=2,
        grid=(n // tm,),
        in_specs=[
            pl.BlockSpec((tm, d), lambda i, *_: (i, 0)),
            pl.BlockSpec((tm, 8), lambda i, *_: (i, 0)),
            pl.BlockSpec(memory_space=pl.ANY),
        ],
        out_specs=pl.BlockSpec((tm, d), lambda i, *_: (i, 0)),
        scratch_shapes=[
            pltpu.VMEM((tm, half), jnp.uint32),
            pltpu.VMEM((tm, half), jnp.uint32),
            pltpu.SemaphoreType.DMA,
        ],
    )
    return pl.pallas_call(
        functools.partial(_combine_kernel, tm=tm),
        grid_spec=grid_spec,
        out_shape=jax.ShapeDtypeStruct((n, d), F32),
        compiler_params=_cparams(("arbitrary",)),
        name="combine",
    )(d1, d2, h1, wcol, ys)


def _rope_tab(pos):
    inv = 1.0 / (ROPE_THETA ** (jnp.arange(0, QK_ROPE, 2, dtype=F32) / QK_ROPE))
    ang = pos.astype(F32)[:, None] * inv[None, :]
    c = jnp.cos(ang)
    s = jnp.sin(ang)
    cos64 = jnp.concatenate([c, c], axis=-1)
    sin64 = jnp.concatenate([-s, s], axis=-1)
    tab_q = jnp.concatenate([cos64, sin64], axis=-1)
    tab_k = jnp.concatenate([cos64, cos64, sin64, sin64], axis=-1)
    return tab_q, tab_k


def _swap_halves(a):
    h = a.shape[-1] // 2
    return jnp.concatenate([a[..., h:], a[..., :h]], axis=-1)


def _tile(n, pref):
    t = pref
    while n % t:
        t //= 2
    return t


def kernel(x, meta_tokens, mix_norm_g, w_in, q_lat_norm_g, w_uq, kv_lat_norm_g, w_ukv, q_head_norm_g,
           k_head_norm_g, w_pool, pool_scale, w_out, ffn_norm_g, w_group, b_group, w_expert, b_expert,
           w_gate, w_up, w_down):
    batch, seq, d = x.shape
    depth = w_in.shape[0]
    pool_w = pool_scale.shape[-1]
    q_rank = q_lat_norm_g.shape[-1]
    n_heads = w_uq.shape[-1] // QK_HEAD
    n = batch * seq
    f = w_gate.shape[-1]

    tab_q, tab_k = _rope_tab(N_META + jnp.arange(seq))
    tab_qt = tab_q.T
    meta_rows = 128
    _, mtab_k = _rope_tab(jnp.arange(meta_rows))
    tm_exp = _tile(n, 256)
    n_blocks = (2 * n + N_EXPERTS * (tm_exp - 1) + tm_exp - 1) // tm_exp
    n_rows = n_blocks * tm_exp

    h = x.reshape(n, d)
    meta = meta_tokens.astype(F32)
    for l in range(depth):
        k_r = w_in[l][:, pool_w + q_rank + KV_LORA:]
        k_rs = _swap_halves(k_r)
        w_in_p = jnp.concatenate([w_in[l][:, :pool_w + q_rank + KV_LORA], k_r, k_r, k_rs, k_rs],
                                 axis=-1).astype(BF16)
        wq = w_uq[l].reshape(q_rank, n_heads, QK_HEAD)
        wuq_p = jnp.concatenate([wq, _swap_halves(wq[..., QK_NOPE:])], axis=-1)
        wqt = wuq_p.reshape(q_rank, n_heads * HEAD_PAD).T.astype(BF16)
        wkv = w_ukv[l].reshape(KV_LORA, n_heads, QK_NOPE + V_HEAD)
        wk = wkv[..., :QK_NOPE].reshape(KV_LORA, n_heads * QK_NOPE).astype(BF16)
        wvt = wkv[..., QK_NOPE:].reshape(KV_LORA, n_heads * V_HEAD).T.astype(BF16)
        gq = q_head_norm_g[l]
        gk = k_head_norm_g[l]
        gqc = jnp.broadcast_to(jnp.concatenate([gq, _swap_halves(gq[QK_NOPE:])])[:, None], (HEAD_PAD, 128))
        gk_r = gk[QK_NOPE:]
        gkh = jnp.concatenate([gk[:QK_NOPE], gk_r, gk_r, _swap_halves(gk_r), _swap_halves(gk_r)])[None, :]
        wr_t = jnp.zeros((ROUTER_ROWS, d), F32)
        wr_t = wr_t.at[:N_GROUPS].set(w_group[l].T).at[N_GROUPS:N_GROUPS + N_EXPERTS].set(w_expert[l].T)
        br = jnp.zeros((ROUTER_ROWS, 1), F32)
        br = br.at[:N_GROUPS, 0].set(b_group[l]).at[N_GROUPS:N_GROUPS + N_EXPERTS, 0].set(b_expert[l])

        u, lat, mu, mlat = _in_proj(h, meta, mix_norm_g[l][None, :], w_in_p, pool_w, _tile(n, 256))
        mlat_p = jnp.pad(mlat, ((0, meta_rows - N_META), (0, 0)))
        qt, k, vt, mk, mvt = _qkv(lat, tab_k, tab_qt, mlat_p, mtab_k, q_lat_norm_g[l][None, :],
                                  kv_lat_norm_g[l][None, :], gqc, gkh, wqt, wk, wvt,
                                  batch, seq, n_heads, q_rank, _tile(seq, 256))
        y_mla = _attention(qt, k, vt, mk, mvt, _tile(seq, 512)).reshape(n, n_heads * V_HEAD)
        h1 = _out_proj(u, mu, y_mla, h, w_pool[l].astype(BF16), pool_scale[l][None, :],
                       w_out[l].astype(BF16), seq, _tile(seq, 256), _tile(d, 2048))

        xp, ri, rf, cnt = _router(h1, ffn_norm_g[l][None, :], wr_t, br, _tile(n, 512))
        counts = cnt[:, 0].astype(jnp.int32)
        padded = (counts + tm_exp - 1) // tm_exp * tm_exp
        pend = jnp.cumsum(padded)
        pstart = pend - padded
        n_used = (pend[-1] // tm_exp).astype(jnp.int32)
        blk_start = jnp.arange(n_blocks, dtype=jnp.int32) * tm_exp
        blk_expert = jnp.minimum(jnp.searchsorted(pend, blk_start, side="right"), N_EXPERTS - 1)
        blk_expert = blk_expert.astype(jnp.int32)
        npad = (pend[-1] - 2 * n).astype(jnp.int32)

        dest = _dest(ri, pstart.astype(F32)[:, None], _tile(n, 512))
        d1 = dest[0]
        d2 = dest[1]
        xs = _dispatch(d1, d2, counts, pstart.astype(jnp.int32), padded.astype(jnp.int32),
                       npad[None], n_used[None], xp, n_rows, _tile(n, 256), tm_exp)
        ys = _experts(blk_expert, n_used[None], xs, w_gate[l].astype(BF16), w_up[l].astype(BF16),
                      w_down[l].astype(BF16), tm_exp)
        wcol = rf.T
        h = _combine(d1, d2, h1, wcol, ys, _tile(n, 256))
        if l + 1 < depth:
            raise NotImplementedError("meta-token outputs are only produced for a single layer")
    return h.reshape(batch, seq, d)
```

```python
import functools

import jax
import jax.numpy as jnp
from jax import lax
from jax.experimental import pallas as pl
from jax.experimental.pallas import tpu as pltpu

F32 = jnp.float32
BF16 = jnp.bfloat16

N_META = 16
POOL_WINDOWS = (2, 4, 8, 16)
QK_NOPE = 128
QK_ROPE = 64
QK_HEAD = QK_NOPE + QK_ROPE
V_HEAD = 128
V_AUG = V_HEAD + 16
LOG2E = 1.4426950408889634
HEAD_PAD = 256
KV_LORA = 512
ROPE_THETA = 10000.0
N_GROUPS = 8
EXPERTS_PER_GROUP = 8
N_EXPERTS = N_GROUPS * EXPERTS_PER_GROUP
EPS = 1e-6
ROUTER_ROWS = 128
NEG_BIG = -1e30

VMEM_LIMIT = 56 * 1024 * 1024


def _cparams(sem, vmem=VMEM_LIMIT):
    return pltpu.CompilerParams(dimension_semantics=sem, vmem_limit_bytes=vmem)


def _rms(xf, g, n):
    ss = jnp.sum(xf * xf, axis=-1, keepdims=True)
    return xf * lax.rsqrt(ss * (1.0 / n) + EPS) * g


def _in_proj_kernel(x_ref, meta_ref, g_ref, w_ref, u_ref, lat_ref, mu_ref, mlat_ref, *, n_u, d):
    def proj(xf):
        xn = _rms(xf, g_ref[...], d).astype(BF16)
        return jnp.dot(xn, w_ref[...], preferred_element_type=F32)

    p = proj(x_ref[...])
    u_ref[...] = p[:, :n_u]
    lat_ref[...] = p[:, n_u:]

    @pl.when(pl.program_id(0) == 0)
    def _():
        pm = proj(meta_ref[...])
        mu_ref[...] = pm[:, :n_u]
        mlat_ref[...] = pm[:, n_u:]


def _in_proj(x2d, meta, g, w_p, n_u, tm):
    n, d = x2d.shape
    n_out = w_p.shape[1]
    n_lat = n_out - n_u
    return pl.pallas_call(
        functools.partial(_in_proj_kernel, n_u=n_u, d=d),
        grid=(n // tm,),
        in_specs=[
            pl.BlockSpec((tm, d), lambda i: (i, 0)),
            pl.BlockSpec((N_META, d), lambda i: (0, 0)),
            pl.BlockSpec((1, d), lambda i: (0, 0)),
            pl.BlockSpec((d, n_out), lambda i: (0, 0), pipeline_mode=pl.Buffered(1)),
        ],
        out_specs=[
            pl.BlockSpec((tm, n_u), lambda i: (i, 0)),
            pl.BlockSpec((tm, n_lat), lambda i: (i, 0)),
            pl.BlockSpec((N_META, n_u), lambda i: (0, 0)),
            pl.BlockSpec((N_META, n_lat), lambda i: (0, 0)),
        ],
        out_shape=[
            jax.ShapeDtypeStruct((n, n_u), F32),
            jax.ShapeDtypeStruct((n, n_lat), F32),
            jax.ShapeDtypeStruct((N_META, n_u), F32),
            jax.ShapeDtypeStruct((N_META, n_lat), F32),
        ],
        compiler_params=_cparams(("arbitrary",)),
        name="in_proj",
    )(x2d, meta, g, w_p)


def _lanes(a, m):
    return jnp.concatenate([a] * (m // 128), axis=1) if m > 128 else a


def _ones_rows(r, m):
    return jnp.where(lax.broadcasted_iota(jnp.int32, (r, m), 0) == 0, 1.0, 0.0).astype(BF16)


def _kv_rows(lat, tab, gkvl_ref, gkh_ref, wk_ref, wvt_ref, store_k, store_vt, *, n_heads, q_rank):
    kvl = lat[:, q_rank:q_rank + KV_LORA]
    kd = lat[:, q_rank + KV_LORA:q_rank + KV_LORA + 128]
    kr = lat[:, q_rank + KV_LORA + 128:q_rank + KV_LORA + 256]
    kvn = _rms(kvl, gkvl_ref[...], KV_LORA)
    kvn_b = kvn.astype(BF16)
    kvn_t = kvn.T.astype(BF16)
    gk0 = gkh_ref[:, :128]
    k_rope = kd * gkh_ref[:, 128:256] * tab[:, :128] + kr * gkh_ref[:, 256:384] * tab[:, 128:256]
    ss_kr = 0.5 * jnp.sum(kd * kd, axis=-1, keepdims=True)
    inv_n = 1.0 / QK_HEAD
    for p in range(n_heads // 2):
        kk = jnp.dot(kvn_b, wk_ref[:, p * 256:(p + 1) * 256], preferred_element_type=F32)
        vv = jnp.dot(wvt_ref[p * 256:(p + 1) * 256, :], kvn_t, preferred_element_type=F32)
        for t in range(2):
            kn = kk[:, t * 128:(t + 1) * 128]
            ssk = jnp.sum(kn * kn, axis=-1, keepdims=True) + ss_kr
            rk = lax.rsqrt(ssk * inv_n + EPS)
            store_k(2 * p + t, jnp.concatenate([kn * rk * gk0, k_rope * rk], axis=-1).astype(BF16))
            store_vt(2 * p + t, vv[t * 128:(t + 1) * 128].astype(BF16))


def _qkv_kernel(lat_ref, tab_ref, tqt_ref, mlat_ref, mtab_ref, gql_ref, gkvl_ref, gqc_ref, gkh_ref,
                wqt_ref, wk_ref, wvt_ref, qt_ref, k_ref, vt_ref, mk_ref, mvt_ref, *, n_heads, q_rank):
    lat = lat_ref[...]
    m = lat.shape[0]

    def sk(h, val):
        k_ref[h] = val

    def svt(h, val):
        vt_ref[h, :V_HEAD, :] = val
        vt_ref[h, V_HEAD:, :] = _ones_rows(V_AUG - V_HEAD, val.shape[1])

    _kv_rows(lat, tab_ref[...], gkvl_ref, gkh_ref, wk_ref, wvt_ref, sk, svt, n_heads=n_heads, q_rank=q_rank)

    scale = QK_HEAD ** -0.5 * LOG2E
    qn_t = _rms(lat[:, :q_rank], gql_ref[...], q_rank).T.astype(BF16)
    gq0 = _lanes(gqc_ref[:128, :], m) * scale
    gq1 = _lanes(gqc_ref[128:, :], m) * tqt_ref[...] * scale
    inv_n = 1.0 / QK_HEAD
    for h in range(n_heads):
        xh = jnp.dot(wqt_ref[h * HEAD_PAD:(h + 1) * HEAD_PAD, :], qn_t, preferred_element_type=F32)
        x0 = xh[:128]
        x1 = xh[128:]
        ss = jnp.sum(x0 * x0, axis=0, keepdims=True) + 0.5 * jnp.sum(x1 * x1, axis=0, keepdims=True)
        rinv = lax.rsqrt(ss * inv_n + EPS)
        qt_ref[h] = jnp.concatenate([x0 * rinv * gq0, x1 * rinv * gq1], axis=0).astype(BF16)

    @pl.when((pl.program_id(0) == 0) & (pl.program_id(1) == 0))
    def _():
        def smk(h, val):
            mk_ref[h] = val

        def smvt(h, val):
            mvt_ref[h, :V_HEAD, :] = val
            mvt_ref[h, V_HEAD:, :] = _ones_rows(V_AUG - V_HEAD, val.shape[1])

        _kv_rows(mlat_ref[...], mtab_ref[...], gkvl_ref, gkh_ref, wk_ref, wvt_ref, smk, smvt,
                 n_heads=n_heads, q_rank=q_rank)


def _qkv(lat, tab, tqt, mlat, mtab, gql, gkvl, gqc, gkh, wqt, wk, wvt, batch, seq, n_heads, q_rank, tm):
    n_lat = lat.shape[1]
    tpb = seq // tm
    mp = mlat.shape[0]
    const = lambda b, i: (0, 0)
    return pl.pallas_call(
        functools.partial(_qkv_kernel, n_heads=n_heads, q_rank=q_rank),
        grid=(batch, tpb),
        in_specs=[
            pl.BlockSpec((tm, n_lat), lambda b, i: (b * tpb + i, 0)),
            pl.BlockSpec((tm, 256), lambda b, i: (i, 0)),
            pl.BlockSpec((128, tm), lambda b, i: (0, i)),
            pl.BlockSpec((mp, n_lat), const),
            pl.BlockSpec((mp, 256), const),
            pl.BlockSpec((1, q_rank), const),
            pl.BlockSpec((1, KV_LORA), const),
            pl.BlockSpec((256, 128), const),
            pl.BlockSpec((1, 384), const),
            pl.BlockSpec(wqt.shape, const, pipeline_mode=pl.Buffered(1)),
            pl.BlockSpec(wk.shape, const, pipeline_mode=pl.Buffered(1)),
            pl.BlockSpec(wvt.shape, const, pipeline_mode=pl.Buffered(1)),
        ],
        out_specs=[
            pl.BlockSpec((None, n_heads, HEAD_PAD, tm), lambda b, i: (b, 0, 0, i)),
            pl.BlockSpec((None, n_heads, tm, HEAD_PAD), lambda b, i: (b, 0, i, 0)),
            pl.BlockSpec((None, n_heads, None, V_AUG, tm), lambda b, i: (b, 0, i, 0, 0)),
            pl.BlockSpec((n_heads, mp, HEAD_PAD), lambda b, i: (0, 0, 0)),
            pl.BlockSpec((n_heads, V_AUG, mp), lambda b, i: (0, 0, 0)),
        ],
        out_shape=[
            jax.ShapeDtypeStruct((batch, n_heads, HEAD_PAD, seq), BF16),
            jax.ShapeDtypeStruct((batch, n_heads, seq, HEAD_PAD), BF16),
            jax.ShapeDtypeStruct((batch, n_heads, tpb, V_AUG, tm), BF16),
            jax.ShapeDtypeStruct((n_heads, mp, HEAD_PAD), BF16),
            jax.ShapeDtypeStruct((n_heads, V_AUG, mp), BF16),
        ],
        compiler_params=_cparams(("arbitrary", "arbitrary")),
        name="qkv",
    )(lat, tab, tqt, mlat, mtab, gql, gkvl, gqc, gkh, wqt, wk, wvt)


ATTN_HEADS = 2
ATTN_STRIP = 256
ATTN_TQ = 2048
ATTN_TK = 512


def _attn_kernel(qt_ref, k_ref, vt_ref, mk_ref, mvt_ref, o_ref, acc_ref, *, tq, tv):
    qi = pl.program_id(2)
    nh = qt_ref.shape[0]
    sw = ATTN_STRIP
    chains = [(hh, x) for x in range(tq // sw) for hh in range(nh)]
    qts = [qt_ref[hh, :, x * sw:(x + 1) * sw] for hh, x in chains]

    carry = []
    for ci, (hh, x) in enumerate(chains):
        s0 = jnp.dot(mk_ref[hh], qts[ci], preferred_element_type=F32)
        s0 = jnp.where(lax.broadcasted_iota(jnp.int32, s0.shape, 0) < N_META, s0, NEG_BIG)
        m0 = jnp.max(s0, axis=0, keepdims=True)
        carry.append(m0)
        acc_ref[hh, :, x * sw:(x + 1) * sw] = jnp.dot(mvt_ref[hh], jnp.exp2(s0 - m0).astype(BF16),
                                                     preferred_element_type=F32)

    tk = ATTN_TK
    kpq = tq // tk

    def n_keys(x, diag):
        if diag is None:
            return tk
        return max(0, min(tk, (x + 1) * sw - diag * tk))

    def scores(ci, j, diag):
        hh, x = chains[ci]
        start = pl.multiple_of(j * tk, tk)
        s = jnp.dot(k_ref[hh, pl.ds(start, n_keys(x, diag)), :], qts[ci],
                    preferred_element_type=F32)
        if diag is not None and (diag + 1) * tk - 1 > x * sw:
            key = lax.broadcasted_iota(jnp.int32, s.shape, 0) + diag * tk
            qry = lax.broadcasted_iota(jnp.int32, s.shape, 1) + x * sw
            s = jnp.where(key <= qry, s, NEG_BIG)
        return s

    def softmax(s, m_prev):
        m_new = jnp.maximum(m_prev, jnp.max(s, axis=0, keepdims=True))
        return m_new, jnp.exp2(m_prev - m_new), jnp.exp2((s - m_new).astype(BF16))

    def values(ci, j, alpha, pb):
        hh, x = chains[ci]
        pv = jnp.dot(vt_ref[hh, j * (tk // tv)], pb[:tv], preferred_element_type=F32)
        for c in range(1, pb.shape[0] // tv):
            pv += jnp.dot(vt_ref[hh, j * (tk // tv) + c], pb[c * tv:(c + 1) * tv],
                          preferred_element_type=F32)
        acc_ref[hh, :, x * sw:(x + 1) * sw] = alpha * acc_ref[hh, :, x * sw:(x + 1) * sw] + pv

    def step(j, c, diag=None):
        live = [ci for ci, (hh, x) in enumerate(chains) if n_keys(x, diag) > 0]
        n = len(live)
        out = list(c)
        s_vals = {}
        sm_vals = {}
        for t in range(n + 2):
            if t < n:
                s_vals[t] = scores(live[t], j, diag)
            if 0 <= t - 1 < n:
                ci = live[t - 1]
                out[ci], alpha, pb = softmax(s_vals.pop(t - 1), c[ci])
                sm_vals[t - 1] = (alpha, pb)
            if 0 <= t - 2 < n:
                values(live[t - 2], j, *sm_vals.pop(t - 2))
        return tuple(out)

    c = lax.fori_loop(0, qi * kpq, lambda j, c: step(j, c), tuple(carry))
    for diag in range(kpq):
        c = step(qi * kpq + diag, c, diag)
    for hh, x in chains:
        acc = acc_ref[hh, :, x * sw:(x + 1) * sw]
        o_ref[x * sw:(x + 1) * sw, hh * V_HEAD:(hh + 1) * V_HEAD] = (
            acc[:V_HEAD] / acc[V_HEAD:V_HEAD + 1]).T.astype(o_ref.dtype)


def _attention(qt, k, vt, mk, mvt, tq):
    batch, n_heads, seq, _ = k.shape
    n_chunks, tv = vt.shape[2], vt.shape[4]
    mp = mk.shape[1]
    nh = ATTN_HEADS
    return pl.pallas_call(
        functools.partial(_attn_kernel, tq=tq, tv=tv),
        grid=(batch, n_heads // nh, seq // tq),
        in_specs=[
            pl.BlockSpec((None, nh, HEAD_PAD, tq), lambda b, h, i: (b, h, 0, i)),
            pl.BlockSpec((None, nh, seq, HEAD_PAD), lambda b, h, i: (b, h, 0, 0)),
            pl.BlockSpec((None, nh, n_chunks, V_AUG, tv), lambda b, h, i: (b, h, 0, 0, 0)),
            pl.BlockSpec((nh, mp, HEAD_PAD), lambda b, h, i: (h, 0, 0)),
            pl.BlockSpec((nh, V_AUG, mp), lambda b, h, i: (h, 0, 0)),
        ],
        out_specs=pl.BlockSpec((None, tq, nh * V_HEAD), lambda b, h, i: (b, i, h)),
        out_shape=jax.ShapeDtypeStruct((batch, seq, n_heads * V_HEAD), BF16),
        scratch_shapes=[pltpu.VMEM((nh, V_AUG, tq), F32)],
        compiler_params=_cparams(("arbitrary", "arbitrary", "arbitrary")),
        name="attention",
    )(qt, k, vt, mk, mvt)


def _out_proj_kernel(u_ref, halo_ref, mu_ref, y_ref, x_ref, wp_ref, ps_ref, wo_ref, o_ref,
                     *, tiles_per_batch, group):
    i = pl.program_id(1)
    first = (i % tiles_per_batch) == 0
    halo = jnp.where(first, mu_ref[...], halo_ref[...])
    ucat = jnp.concatenate([halo, u_ref[...]], axis=0)
    ys = []
    for gi, w in enumerate(POOL_WINDOWS):
        ug = ucat[:, gi * group:(gi + 1) * group]
        s = ug
        shift = 1
        while shift < w:
            s = s + pltpu.roll(s, shift, 0)
            shift *= 2
        diff = (s * (1.0 / w) - ug)[N_META:]
        yg = jnp.dot(diff.astype(BF16), wp_ref[gi], preferred_element_type=F32)
        ys.append((yg * ps_ref[:, gi * group:(gi + 1) * group]).astype(BF16))
    ycat = jnp.concatenate(ys + [y_ref[...]], axis=-1)
    o_ref[...] = x_ref[...] + jnp.dot(ycat, wo_ref[...], preferred_element_type=F32)


def _out_proj(u, mu, y_mla, x2d, w_pool, pool_scale, w_out, seq, tm, tn):
    n, d = x2d.shape
    pool_w = u.shape[1]
    mla_w = y_mla.shape[1]
    group = pool_w // len(POOL_WINDOWS)
    hb = tm // N_META
    return pl.pallas_call(
        functools.partial(_out_proj_kernel, tiles_per_batch=seq // tm, group=group),
        grid=(d // tn, n // tm),
        in_specs=[
            pl.BlockSpec((tm, pool_w), lambda j, i: (i, 0)),
            pl.BlockSpec((N_META, pool_w), lambda j, i: (jnp.maximum(i * hb - 1, 0), 0)),
            pl.BlockSpec((N_META, pool_w), lambda j, i: (0, 0)),
            pl.BlockSpec((tm, mla_w), lambda j, i: (i, 0)),
            pl.BlockSpec((tm, tn), lambda j, i: (i, j)),
            pl.BlockSpec(w_pool.shape, lambda j, i: (0, 0, 0)),
            pl.BlockSpec((1, pool_w), lambda j, i: (0, 0)),
            pl.BlockSpec((d, tn), lambda j, i: (0, j)),
        ],
        out_specs=pl.BlockSpec((tm, tn), lambda j, i: (i, j)),
        out_shape=jax.ShapeDtypeStruct((n, d), F32),
        compiler_params=_cparams(("arbitrary", "arbitrary")),
        name="out_proj",
    )(u, u, mu, y_mla, x2d, w_pool, pool_scale, w_out)


def _pack_bf16_pair(lo, hi):
    lo_b = lax.bitcast_convert_type(lo.astype(BF16).astype(F32), jnp.uint32)
    hi_b = lax.bitcast_convert_type(hi.astype(BF16).astype(F32), jnp.uint32)
    return lax.shift_right_logical(lo_b, jnp.uint32(16)) | hi_b


def _unpack_bf16_pair(w):
    lo = lax.bitcast_convert_type(lax.shift_left(w, jnp.uint32(16)), F32)
    hi = lax.bitcast_convert_type(w & jnp.uint32(0xFFFF0000), F32)
    return lo, hi


def _first_argmax(vals, sub, n):
    mx = jnp.max(vals, axis=0, keepdims=True)
    idx = jnp.min(jnp.where(vals == mx, sub, n), axis=0, keepdims=True)
    return mx, idx


def _router_kernel(h_ref, g_ref, wr_ref, br_ref, xp_ref, ri_ref, rf_ref, cnt_ref, *, d):
    xn = _rms(h_ref[...], g_ref[...], d)
    half = d // 2
    xp_ref[...] = _pack_bf16_pair(xn[:, :half], xn[:, half:])
    logits = lax.dot_general(wr_ref[...], xn.astype(BF16), (((1,), (1,)), ((), ())),
                             preferred_element_type=F32) + br_ref[...]
    tm = logits.shape[1]
    sub = lax.broadcasted_iota(jnp.int32, (N_GROUPS, tm), 0)
    lg = logits[:N_GROUPS]
    gmax, g_idx = _first_argmax(lg, sub, N_GROUPS)
    g_p = 1.0 / jnp.sum(jnp.exp(lg - gmax), axis=0, keepdims=True)
    in_group = jnp.zeros((EXPERTS_PER_GROUP, tm), F32)
    for g in range(N_GROUPS):
        lo = N_GROUPS + g * EXPERTS_PER_GROUP
        in_group = jnp.where(g_idx == g, logits[lo:lo + EXPERTS_PER_GROUP], in_group)
    m1, i1 = _first_argmax(in_group, sub, EXPERTS_PER_GROUP)
    rest = jnp.where(sub == i1, -jnp.inf, in_group)
    m2, i2 = _first_argmax(rest, sub, EXPERTS_PER_GROUP)
    e = jnp.exp(m2 - m1)
    w1 = g_p / (1.0 + e)
    w2 = g_p * e / (1.0 + e)
    e1 = g_idx * EXPERTS_PER_GROUP + i1
    e2 = g_idx * EXPERTS_PER_GROUP + i2
    ri_ref[...] = jnp.where(sub == 0, e1, jnp.where(sub == 1, e2, 0))
    rf_ref[...] = jnp.where(sub == 0, w1, jnp.where(sub == 1, w2, 0.0))
    sub_e = lax.broadcasted_iota(jnp.int32, (N_EXPERTS, tm), 0)
    hit = jnp.where((sub_e == e1) | (sub_e == e2), 1.0, 0.0)
    cnt = jnp.sum(hit, axis=1, keepdims=True)

    @pl.when(pl.program_id(0) == 0)
    def _():
        cnt_ref[...] = jnp.zeros_like(cnt_ref)

    cnt_ref[...] += jnp.broadcast_to(cnt, cnt_ref.shape)


def _router(h1, g, wr_t, br, tm):
    n, d = h1.shape
    return pl.pallas_call(
        functools.partial(_router_kernel, d=d),
        grid=(n // tm,),
        in_specs=[
            pl.BlockSpec((tm, d), lambda i: (i, 0)),
            pl.BlockSpec((1, d), lambda i: (0, 0)),
            pl.BlockSpec((ROUTER_ROWS, d), lambda i: (0, 0)),
            pl.BlockSpec((ROUTER_ROWS, 1), lambda i: (0, 0)),
        ],
        out_specs=[
            pl.BlockSpec((tm, d // 2), lambda i: (i, 0)),
            pl.BlockSpec((8, tm), lambda i: (0, i)),
            pl.BlockSpec((8, tm), lambda i: (0, i)),
            pl.BlockSpec((N_EXPERTS, 128), lambda i: (0, 0)),
        ],
        out_shape=[
            jax.ShapeDtypeStruct((n, d // 2), jnp.uint32),
            jax.ShapeDtypeStruct((8, n), jnp.int32),
            jax.ShapeDtypeStruct((8, n), F32),
            jax.ShapeDtypeStruct((N_EXPERTS, 128), F32),
        ],
        compiler_params=_cparams(("arbitrary",)),
        name="router",
    )(h1, g, wr_t, br)


def _dest_kernel(ri_ref, pstart_ref, tri_ref, d_ref, carry_ref):
    @pl.when(pl.program_id(0) == 0)
    def _():
        carry_ref[...] = jnp.zeros_like(carry_ref)

    tn = ri_ref.shape[1]
    e1 = ri_ref[0:1, :]
    e2 = ri_ref[1:2, :]
    sub = lax.broadcasted_iota(jnp.int32, (N_EXPERTS, tn), 0)
    oh1 = sub == e1
    oh2 = sub == e2
    oh = jnp.where(oh1 | oh2, 1.0, 0.0)
    c = jnp.dot(oh.astype(BF16), tri_ref[...], preferred_element_type=F32)
    val = c + (pstart_ref[...] + carry_ref[:, :1] - 1.0)
    d1 = jnp.sum(jnp.where(oh1, val, 0.0), axis=0, keepdims=True)
    d2 = jnp.sum(jnp.where(oh2, val, 0.0), axis=0, keepdims=True)
    sub8 = lax.broadcasted_iota(jnp.int32, (8, tn), 0)
    d_ref[...] = jnp.where(sub8 == 0, d1, jnp.where(sub8 == 1, d2, 0.0)).astype(jnp.int32)
    carry_ref[...] += jnp.broadcast_to(jnp.sum(oh, axis=1, keepdims=True), carry_ref.shape)


def _dest(ri, pstart_col, tn):
    n = ri.shape[1]
    tri = (jnp.arange(tn)[:, None] <= jnp.arange(tn)[None, :]).astype(BF16)
    return pl.pallas_call(
        _dest_kernel,
        grid=(n // tn,),
        in_specs=[
            pl.BlockSpec((8, tn), lambda i: (0, i)),
            pl.BlockSpec((N_EXPERTS, 1), lambda i: (0, 0)),
            pl.BlockSpec((tn, tn), lambda i: (0, 0)),
        ],
        out_specs=pl.BlockSpec((8, tn), lambda i: (0, i)),
        out_shape=jax.ShapeDtypeStruct((8, n), jnp.int32),
        scratch_shapes=[pltpu.VMEM((N_EXPERTS, 128), F32)],
        compiler_params=_cparams(("arbitrary",)),
        name="dest",
    )(ri, pstart_col, tri)


def _dispatch_kernel(d1_ref, d2_ref, cnt_ref, pstart_ref, padded_ref, npad_ref, nused_ref,
                     xp_ref, xs_ref, zero_ref, sem, zsem, bsem, *, tm, tb, n_blocks):
    i = pl.program_id(0)
    base = i * tm

    def row_copy(r, dst):
        return pltpu.make_async_copy(xp_ref.at[pl.ds(r, 1), :], xs_ref.at[pl.ds(dst, 1), :], sem)

    def zero_row_copy(dst):
        return pltpu.make_async_copy(zero_ref.at[pl.ds(0, 1), :], xs_ref.at[pl.ds(dst, 1), :], zsem)

    def zero_block_copy(blk):
        start = pl.multiple_of(blk * tb, tb)
        return pltpu.make_async_copy(zero_ref, xs_ref.at[pl.ds(start, tb), :], bsem)

    def issue(r, c):
        row_copy(r, d1_ref[base + r]).start()
        row_copy(r, d2_ref[base + r]).start()
        return c

    lax.fori_loop(0, tm, issue, 0, unroll=4)

    @pl.when(i == 0)
    def _():
        zero_ref[...] = jnp.zeros_like(zero_ref)

        def per_expert(e, c):
            lo = pstart_ref[e] + cnt_ref[e]
            hi = pstart_ref[e] + padded_ref[e]

            def per_row(p, c2):
                zero_row_copy(p).start()
                return c2

            lax.fori_loop(lo, hi, per_row, 0)
            return c

        lax.fori_loop(0, N_EXPERTS, per_expert, 0)

        def per_block(b, c):
            zero_block_copy(b).start()
            return c

        lax.fori_loop(nused_ref[0], n_blocks, per_block, 0)

        def zwait(p, c):
            zero_row_copy(0).wait()
            return c

        lax.fori_loop(0, npad_ref[0], zwait, 0)

        def bwait(b, c):
            zero_block_copy(0).wait()
            return c

        lax.fori_loop(nused_ref[0], n_blocks, bwait, 0)

    for _ in range(2):
        pltpu.make_async_copy(xp_ref, xs_ref.at[pl.ds(0, tm), :], sem).wait()


def _dispatch(d1, d2, cnt, pstart, padded, npad, n_used, xp, n_rows, tm, tb):
    n, half = xp.shape
    grid_spec = pltpu.PrefetchScalarGridSpec(
        num_scalar_prefetch=7,
        grid=(n // tm,),
        in_specs=[pl.BlockSpec((tm, half), lambda i, *_: (i, 0))],
        out_specs=pl.BlockSpec(memory_space=pl.ANY),
        scratch_shapes=[
            pltpu.VMEM((tb, half), jnp.uint32),
            pltpu.SemaphoreType.DMA,
            pltpu.SemaphoreType.DMA,
            pltpu.SemaphoreType.DMA,
        ],
    )
    return pl.pallas_call(
        functools.partial(_dispatch_kernel, tm=tm, tb=tb, n_blocks=n_rows // tb),
        grid_spec=grid_spec,
        out_shape=jax.ShapeDtypeStruct((n_rows, half), jnp.uint32),
        compiler_params=pltpu.CompilerParams(
            dimension_semantics=("arbitrary",), vmem_limit_bytes=VMEM_LIMIT, has_side_effects=True),
        name="dispatch",
    )(d1, d2, cnt, pstart, padded, npad, n_used, xp)


def _experts_kernel(be_ref, nused_ref, xs_ref, wg_ref, wu_ref, wd_ref, ys_ref):
    i = pl.program_id(0)

    @pl.when(i < nused_ref[0])
    def _():
        lo, hi = _unpack_bf16_pair(xs_ref[...])
        half = lo.shape[1]
        g = (jnp.dot(lo, wg_ref[:half], preferred_element_type=F32)
             + jnp.dot(hi, wg_ref[half:], preferred_element_type=F32))
        u = (jnp.dot(lo, wu_ref[:half], preferred_element_type=F32)
             + jnp.dot(hi, wu_ref[half:], preferred_element_type=F32))
        hb = g / (1.0 + jnp.exp(-g)) * u
        y = jnp.dot(hb, wd_ref[...], preferred_element_type=F32)
        ys_ref[...] = _pack_bf16_pair(y[:, :half], y[:, half:])

    @pl.when(i >= nused_ref[0])
    def _():
        ys_ref[...] = jnp.zeros_like(ys_ref)


def _experts(blk_expert, n_used, xs, wg, wu, wd, tm):
    n_rows, half = xs.shape
    _, d, f = wg.shape
    n_blocks = n_rows // tm

    def row_map(i, be, nu):
        return (i, 0)

    def w_map(i, be, nu):
        return (be[i], 0, 0)

    grid_spec = pltpu.PrefetchScalarGridSpec(
        num_scalar_prefetch=2,
        grid=(n_blocks,),
        in_specs=[
            pl.BlockSpec((tm, half), row_map),
            pl.BlockSpec((None, d, f), w_map),
            pl.BlockSpec((None, d, f), w_map),
            pl.BlockSpec((None, f, d), w_map, pipeline_mode=pl.Buffered(1)),
        ],
        out_specs=pl.BlockSpec((tm, half), row_map),
    )
    return pl.pallas_call(
        _experts_kernel,
        grid_spec=grid_spec,
        out_shape=jax.ShapeDtypeStruct((n_rows, half), jnp.uint32),
        compiler_params=_cparams(("arbitrary",)),
        name="experts",
    )(blk_expert, n_used, xs, wg, wu, wd)


def _combine_kernel(d1_ref, d2_ref, h_ref, w_ref, ys_ref, o_ref, a_ref, b_ref, sem, *, tm):
    i = pl.program_id(0)
    base = i * tm

    def row_copy(src, r, buf):
        return pltpu.make_async_copy(ys_ref.at[pl.ds(src, 1), :], buf.at[pl.ds(r, 1), :], sem)

    def issue(r, c):
        row_copy(d1_ref[base + r], r, a_ref).start()
        row_copy(d2_ref[base + r], r, b_ref).start()
        return c

    lax.fori_loop(0, tm, issue, 0, unroll=4)

    for buf in (a_ref, b_ref):
        pltpu.make_async_copy(ys_ref.at[pl.ds(0, tm), :], buf, sem).wait()

    half = a_ref.shape[1]
    rows = 8

    def slab(r, c):
        r0 = pl.multiple_of(r * rows, rows)
        a_lo, a_hi = _unpack_bf16_pair(a_ref[pl.ds(r0, rows), :])
        b_lo, b_hi = _unpack_bf16_pair(b_ref[pl.ds(r0, rows), :])
        w1 = w_ref[pl.ds(r0, rows), 0:1]
        w2 = w_ref[pl.ds(r0, rows), 1:2]
        o_ref[pl.ds(r0, rows), :half] = h_ref[pl.ds(r0, rows), :half] + (w1 * a_lo + w2 * b_lo)
        o_ref[pl.ds(r0, rows), half:] = h_ref[pl.ds(r0, rows), half:] + (w1 * a_hi + w2 * b_hi)
        return c

    lax.fori_loop(0, tm // rows, slab, 0, unroll=2)


def _combine(d1, d2, h1, wcol, ys, tm):
    n, d = h1.shape
    half = d // 2
    grid_spec = pltpu.PrefetchScalarGridSpec(
        num_scalar_prefetch=2,
        grid=(n // tm,),
        in_specs=[
            pl.BlockSpec((tm, d), lambda i, *_: (i, 0)),
            pl.BlockSpec((tm, 8), lambda i, *_: (i, 0)),
            pl.BlockSpec(memory_space=pl.ANY),
        ],
        out_specs=pl.BlockSpec((tm, d), lambda i, *_: (i, 0)),
        scratch_shapes=[
            pltpu.VMEM((tm, half), jnp.uint32),
            pltpu.VMEM((tm, half), jnp.uint32),
            pltpu.SemaphoreType.DMA,
        ],
    )
    return pl.pallas_call(
        functools.partial(_combine_kernel, tm=tm),
        grid_spec=grid_spec,
        out_shape=jax.ShapeDtypeStruct((n, d), F32),
        compiler_params=_cparams(("arbitrary",)),
        name="combine",
    )(d1, d2, h1, wcol, ys)


def _rope_tab(pos):
    inv = 1.0 / (ROPE_THETA ** (jnp.arange(0, QK_ROPE, 2, dtype=F32) / QK_ROPE))
    ang = pos.astype(F32)[:, None] * inv[None, :]
    c = jnp.cos(ang)
    s = jnp.sin(ang)
    cos64 = jnp.concatenate([c, c], axis=-1)
    sin64 = jnp.concatenate([-s, s], axis=-1)
    tab_q = jnp.concatenate([cos64, sin64], axis=-1)
    tab_k = jnp.concatenate([cos64, cos64, sin64, sin64], axis=-1)
    return tab_q, tab_k


def _swap_halves(a):
    h = a.shape[-1] // 2
    return jnp.concatenate([a[..., h:], a[..., :h]], axis=-1)


def _tile(n, pref):
    t = pref
    while n % t:
        t //= 2
    return t


def kernel(x, meta_tokens, mix_norm_g, w_in, q_lat_norm_g, w_uq, kv_lat_norm_g, w_ukv, q_head_norm_g,
           k_head_norm_g, w_pool, pool_scale, w_out, ffn_norm_g, w_group, b_group, w_expert, b_expert,
           w_gate, w_up, w_down):
    batch, seq, d = x.shape
    depth = w_in.shape[0]
    pool_w = pool_scale.shape[-1]
    q_rank = q_lat_norm_g.shape[-1]
    n_heads = w_uq.shape[-1] // QK_HEAD
    n = batch * seq
    f = w_gate.shape[-1]

    tab_q, tab_k = _rope_tab(N_META + jnp.arange(seq))
    tab_qt = tab_q.T
    meta_rows = 128
    _, mtab_k = _rope_tab(jnp.arange(meta_rows))
    tm_exp = _tile(n, 256)
    n_blocks = (2 * n + N_EXPERTS * (tm_exp - 1) + tm_exp - 1) // tm_exp
    n_rows = n_blocks * tm_exp

    h = x.reshape(n, d)
    meta = meta_tokens.astype(F32)
    for l in range(depth):
        k_r = w_in[l][:, pool_w + q_rank + KV_LORA:]
        k_rs = _swap_halves(k_r)
        w_in_p = jnp.concatenate([w_in[l][:, :pool_w + q_rank + KV_LORA], k_r, k_r, k_rs, k_rs],
                                 axis=-1).astype(BF16)
        wq = w_uq[l].reshape(q_rank, n_heads, QK_HEAD)
        wuq_p = jnp.concatenate([wq, _swap_halves(wq[..., QK_NOPE:])], axis=-1)
        wqt = wuq_p.reshape(q_rank, n_heads * HEAD_PAD).T.astype(BF16)
        wkv = w_ukv[l].reshape(KV_LORA, n_heads, QK_NOPE + V_HEAD)
        wk = wkv[..., :QK_NOPE].reshape(KV_LORA, n_heads * QK_NOPE).astype(BF16)
        wvt = wkv[..., QK_NOPE:].reshape(KV_LORA, n_heads * V_HEAD).T.astype(BF16)
        gq = q_head_norm_g[l]
        gk = k_head_norm_g[l]
        gqc = jnp.broadcast_to(jnp.concatenate([gq, _swap_halves(gq[QK_NOPE:])])[:, None], (HEAD_PAD, 128))
        gk_r = gk[QK_NOPE:]
        gkh = jnp.concatenate([gk[:QK_NOPE], gk_r, gk_r, _swap_halves(gk_r), _swap_halves(gk_r)])[None, :]
        wr_t = jnp.zeros((ROUTER_ROWS, d), F32)
        wr_t = wr_t.at[:N_GROUPS].set(w_group[l].T).at[N_GROUPS:N_GROUPS + N_EXPERTS].set(w_expert[l].T)
        br = jnp.zeros((ROUTER_ROWS, 1), F32)
        br = br.at[:N_GROUPS, 0].set(b_group[l]).at[N_GROUPS:N_GROUPS + N_EXPERTS, 0].set(b_expert[l])

        u, lat, mu, mlat = _in_proj(h, meta, mix_norm_g[l][None, :], w_in_p, pool_w, _tile(n, 256))
        mlat_p = jnp.pad(mlat, ((0, meta_rows - N_META), (0, 0)))
        qt, k, vt, mk, mvt = _qkv(lat, tab_k, tab_qt, mlat_p, mtab_k, q_lat_norm_g[l][None, :],
                                  kv_lat_norm_g[l][None, :], gqc, gkh, wqt, wk, wvt,
                                  batch, seq, n_heads, q_rank, _tile(seq, 256))
        y_mla = _attention(qt, k, vt, mk, mvt, _tile(seq, ATTN_TQ)).reshape(n, n_heads * V_HEAD)
        h1 = _out_proj(u, mu, y_mla, h, w_pool[l].astype(BF16), pool_scale[l][None, :],
                       w_out[l].astype(BF16), seq, _tile(seq, 256), _tile(d, 2048))

        xp, ri, rf, cnt = _router(h1, ffn_norm_g[l][None, :], wr_t.astype(BF16), br, _tile(n, 512))
        counts = cnt[:, 0].astype(jnp.int32)
        padded = (counts + tm_exp - 1) // tm_exp * tm_exp
        pend = jnp.cumsum(padded)
        pstart = pend - padded
        n_used = (pend[-1] // tm_exp).astype(jnp.int32)
        blk_start = jnp.arange(n_blocks, dtype=jnp.int32) * tm_exp
        blk_expert = jnp.minimum(jnp.searchsorted(pend, blk_start, side="right"), N_EXPERTS - 1)
        blk_expert = blk_expert.astype(jnp.int32)
        npad = (pend[-1] - 2 * n).astype(jnp.int32)

        dest = _dest(ri, pstart.astype(F32)[:, None], _tile(n, 512))
        d1 = dest[0]
        d2 = dest[1]
        xs = _dispatch(d1, d2, counts, pstart.astype(jnp.int32), padded.astype(jnp.int32),
                       npad[None], n_used[None], xp, n_rows, _tile(n, 256), tm_exp)
        ys = _experts(blk_expert, n_used[None], xs, w_gate[l], w_up[l], w_down[l], tm_exp)
        wcol = rf.T
        h = _combine(d1, d2, h1, wcol, ys, _tile(n, 256))
        if l + 1 < depth:
            raise NotImplementedError("meta-token outputs are only produced for a single layer")
    return h.reshape(batch, seq, d)
```

```python
import functools

import jax
import jax.numpy as jnp
from jax import lax
from jax.experimental import pallas as pl
from jax.experimental.pallas import tpu as pltpu

F32 = jnp.float32
BF16 = jnp.bfloat16

N_META = 16
POOL_WINDOWS = (2, 4, 8, 16)
QK_NOPE = 128
QK_ROPE = 64
QK_HEAD = QK_NOPE + QK_ROPE
V_HEAD = 128
V_AUG = V_HEAD + 16
LOG2E = 1.4426950408889634
HEAD_PAD = 256
KV_LORA = 512
ROPE_THETA = 10000.0
N_GROUPS = 8
EXPERTS_PER_GROUP = 8
N_EXPERTS = N_GROUPS * EXPERTS_PER_GROUP
EPS = 1e-6
ROUTER_ROWS = 128
NEG_BIG = -1e30

VMEM_LIMIT = 56 * 1024 * 1024


def _cparams(sem, vmem=VMEM_LIMIT):
    return pltpu.CompilerParams(dimension_semantics=sem, vmem_limit_bytes=vmem)


def _rms(xf, g, n):
    ss = jnp.sum(xf * xf, axis=-1, keepdims=True)
    return xf * lax.rsqrt(ss * (1.0 / n) + EPS) * g


def _in_proj_kernel(x_ref, meta_ref, g_ref, w_ref, u_ref, lat_ref, mu_ref, mlat_ref, *, n_u, d):
    def proj(xf):
        xn = _rms(xf, g_ref[...], d).astype(BF16)
        return jnp.dot(xn, w_ref[...], preferred_element_type=F32)

    p = proj(x_ref[...])
    u_ref[...] = p[:, :n_u]
    lat_ref[...] = p[:, n_u:]

    @pl.when(pl.program_id(0) == 0)
    def _():
        pm = proj(meta_ref[...])
        mu_ref[...] = pm[:, :n_u]
        mlat_ref[...] = pm[:, n_u:]


def _in_proj(x2d, meta, g, w_p, n_u, tm):
    n, d = x2d.shape
    n_out = w_p.shape[1]
    n_lat = n_out - n_u
    return pl.pallas_call(
        functools.partial(_in_proj_kernel, n_u=n_u, d=d),
        grid=(n // tm,),
        in_specs=[
            pl.BlockSpec((tm, d), lambda i: (i, 0)),
            pl.BlockSpec((N_META, d), lambda i: (0, 0)),
            pl.BlockSpec((1, d), lambda i: (0, 0)),
            pl.BlockSpec((d, n_out), lambda i: (0, 0), pipeline_mode=pl.Buffered(1)),
        ],
        out_specs=[
            pl.BlockSpec((tm, n_u), lambda i: (i, 0)),
            pl.BlockSpec((tm, n_lat), lambda i: (i, 0)),
            pl.BlockSpec((N_META, n_u), lambda i: (0, 0)),
            pl.BlockSpec((N_META, n_lat), lambda i: (0, 0)),
        ],
        out_shape=[
            jax.ShapeDtypeStruct((n, n_u), F32),
            jax.ShapeDtypeStruct((n, n_lat), F32),
            jax.ShapeDtypeStruct((N_META, n_u), F32),
            jax.ShapeDtypeStruct((N_META, n_lat), F32),
        ],
        compiler_params=_cparams(("arbitrary",)),
        name="in_proj",
    )(x2d, meta, g, w_p)


def _lanes(a, m):
    return jnp.concatenate([a] * (m // 128), axis=1) if m > 128 else a


def _ones_rows(r, m):
    return jnp.where(lax.broadcasted_iota(jnp.int32, (r, m), 0) == 0, 1.0, 0.0).astype(BF16)


def _kv_rows(lat, tab, gkvl_ref, gkh_ref, wk_ref, wvt_ref, store_k, store_vt, *, n_heads, q_rank):
    kvl = lat[:, q_rank:q_rank + KV_LORA]
    kd = lat[:, q_rank + KV_LORA:q_rank + KV_LORA + 128]
    kr = lat[:, q_rank + KV_LORA + 128:q_rank + KV_LORA + 256]
    kvn = _rms(kvl, gkvl_ref[...], KV_LORA)
    kvn_b = kvn.astype(BF16)
    kvn_t = kvn.T.astype(BF16)
    gk0 = gkh_ref[:, :128]
    k_rope = kd * gkh_ref[:, 128:256] * tab[:, :128] + kr * gkh_ref[:, 256:384] * tab[:, 128:256]
    ss_kr = 0.5 * jnp.sum(kd * kd, axis=-1, keepdims=True)
    inv_n = 1.0 / QK_HEAD
    for p in range(n_heads // 2):
        kk = jnp.dot(kvn_b, wk_ref[:, p * 256:(p + 1) * 256], preferred_element_type=F32)
        vv = jnp.dot(wvt_ref[p * 256:(p + 1) * 256, :], kvn_t, preferred_element_type=F32)
        for t in range(2):
            kn = kk[:, t * 128:(t + 1) * 128]
            ssk = jnp.sum(kn * kn, axis=-1, keepdims=True) + ss_kr
            rk = lax.rsqrt(ssk * inv_n + EPS)
            store_k(2 * p + t, jnp.concatenate([kn * rk * gk0, k_rope * rk], axis=-1).astype(BF16))
            store_vt(2 * p + t, vv[t * 128:(t + 1) * 128].astype(BF16))


def _qkv_kernel(lat_ref, tab_ref, tqt_ref, mlat_ref, mtab_ref, gql_ref, gkvl_ref, gqc_ref, gkh_ref,
                wqt_ref, wk_ref, wvt_ref, qt_ref, k_ref, vt_ref, mk_ref, mvt_ref, *, n_heads, q_rank):
    lat = lat_ref[...]
    m = lat.shape[0]

    def sk(h, val):
        k_ref[h] = val

    def svt(h, val):
        vt_ref[h, :V_HEAD, :] = val
        vt_ref[h, V_HEAD:, :] = _ones_rows(V_AUG - V_HEAD, val.shape[1])

    _kv_rows(lat, tab_ref[...], gkvl_ref, gkh_ref, wk_ref, wvt_ref, sk, svt, n_heads=n_heads, q_rank=q_rank)

    scale = QK_HEAD ** -0.5 * LOG2E
    qn_t = _rms(lat[:, :q_rank], gql_ref[...], q_rank).T.astype(BF16)
    gq0 = _lanes(gqc_ref[:128, :], m) * scale
    gq1 = _lanes(gqc_ref[128:, :], m) * tqt_ref[...] * scale
    inv_n = 1.0 / QK_HEAD
    for h in range(n_heads):
        xh = jnp.dot(wqt_ref[h * HEAD_PAD:(h + 1) * HEAD_PAD, :], qn_t, preferred_element_type=F32)
        x0 = xh[:128]
        x1 = xh[128:]
        ss = jnp.sum(x0 * x0, axis=0, keepdims=True) + 0.5 * jnp.sum(x1 * x1, axis=0, keepdims=True)
        rinv = lax.rsqrt(ss * inv_n + EPS)
        qt_ref[h] = jnp.concatenate([x0 * rinv * gq0, x1 * rinv * gq1], axis=0).astype(BF16)

    @pl.when((pl.program_id(0) == 0) & (pl.program_id(1) == 0))
    def _():
        def smk(h, val):
            mk_ref[h] = val

        def smvt(h, val):
            mvt_ref[h, :V_HEAD, :] = val
            mvt_ref[h, V_HEAD:, :] = _ones_rows(V_AUG - V_HEAD, val.shape[1])

        _kv_rows(mlat_ref[...], mtab_ref[...], gkvl_ref, gkh_ref, wk_ref, wvt_ref, smk, smvt,
                 n_heads=n_heads, q_rank=q_rank)


def _qkv(lat, tab, tqt, mlat, mtab, gql, gkvl, gqc, gkh, wqt, wk, wvt, batch, seq, n_heads, q_rank, tm):
    n_lat = lat.shape[1]
    tpb = seq // tm
    mp = mlat.shape[0]
    const = lambda b, i: (0, 0)
    return pl.pallas_call(
        functools.partial(_qkv_kernel, n_heads=n_heads, q_rank=q_rank),
        grid=(batch, tpb),
        in_specs=[
            pl.BlockSpec((tm, n_lat), lambda b, i: (b * tpb + i, 0)),
            pl.BlockSpec((tm, 256), lambda b, i: (i, 0)),
            pl.BlockSpec((128, tm), lambda b, i: (0, i)),
            pl.BlockSpec((mp, n_lat), const),
            pl.BlockSpec((mp, 256), const),
            pl.BlockSpec((1, q_rank), const),
            pl.BlockSpec((1, KV_LORA), const),
            pl.BlockSpec((256, 128), const),
            pl.BlockSpec((1, 384), const),
            pl.BlockSpec(wqt.shape, const, pipeline_mode=pl.Buffered(1)),
            pl.BlockSpec(wk.shape, const, pipeline_mode=pl.Buffered(1)),
            pl.BlockSpec(wvt.shape, const, pipeline_mode=pl.Buffered(1)),
        ],
        out_specs=[
            pl.BlockSpec((None, n_heads, HEAD_PAD, tm), lambda b, i: (b, 0, 0, i)),
            pl.BlockSpec((None, n_heads, tm, HEAD_PAD), lambda b, i: (b, 0, i, 0)),
            pl.BlockSpec((None, n_heads, None, V_AUG, tm), lambda b, i: (b, 0, i, 0, 0)),
            pl.BlockSpec((n_heads, mp, HEAD_PAD), lambda b, i: (0, 0, 0)),
            pl.BlockSpec((n_heads, V_AUG, mp), lambda b, i: (0, 0, 0)),
        ],
        out_shape=[
            jax.ShapeDtypeStruct((batch, n_heads, HEAD_PAD, seq), BF16),
            jax.ShapeDtypeStruct((batch, n_heads, seq, HEAD_PAD), BF16),
            jax.ShapeDtypeStruct((batch, n_heads, tpb, V_AUG, tm), BF16),
            jax.ShapeDtypeStruct((n_heads, mp, HEAD_PAD), BF16),
            jax.ShapeDtypeStruct((n_heads, V_AUG, mp), BF16),
        ],
        compiler_params=_cparams(("arbitrary", "arbitrary")),
        name="qkv",
    )(lat, tab, tqt, mlat, mtab, gql, gkvl, gqc, gkh, wqt, wk, wvt)


ATTN_HEADS = 2
ATTN_STRIP = 256
ATTN_TQ = 2048
ATTN_TK = 1024


def _attn_kernel(qt_ref, k_ref, vt_ref, mk_ref, mvt_ref, o_ref, acc_ref, *, tq, tv):
    qi = pl.program_id(2)
    nh = qt_ref.shape[0]
    sw = ATTN_STRIP
    chains = [(hh, x) for x in range(tq // sw) for hh in range(nh)]
    qts = [qt_ref[hh, :, x * sw:(x + 1) * sw] for hh, x in chains]

    carry = []
    for ci, (hh, x) in enumerate(chains):
        s0 = jnp.dot(mk_ref[hh], qts[ci], preferred_element_type=F32)
        s0 = jnp.where(lax.broadcasted_iota(jnp.int32, s0.shape, 0) < N_META, s0, NEG_BIG)
        m0 = jnp.max(s0, axis=0, keepdims=True)
        carry.append(m0)
        acc_ref[hh, :, x * sw:(x + 1) * sw] = jnp.dot(mvt_ref[hh], jnp.exp2(s0 - m0).astype(BF16),
                                                     preferred_element_type=F32)

    tk = ATTN_TK
    kpq = tq // tk

    def n_keys(x, diag):
        if diag is None:
            return tk
        return max(0, min(tk, (x + 1) * sw - diag * tk))

    def scores(ci, j, diag):
        hh, x = chains[ci]
        start = pl.multiple_of(j * tk, tk)
        s = jnp.dot(k_ref[hh, pl.ds(start, n_keys(x, diag)), :], qts[ci],
                    preferred_element_type=F32)
        if diag is not None and (diag + 1) * tk - 1 > x * sw:
            key = lax.broadcasted_iota(jnp.int32, s.shape, 0) + diag * tk
            qry = lax.broadcasted_iota(jnp.int32, s.shape, 1) + x * sw
            s = jnp.where(key <= qry, s, NEG_BIG)
        return s

    def softmax(s, m_prev):
        m_new = jnp.maximum(m_prev, jnp.max(s, axis=0, keepdims=True))
        return m_new, jnp.exp2(m_prev - m_new), jnp.exp2((s - m_new).astype(BF16))

    def values(ci, j, alpha, pb):
        hh, x = chains[ci]
        pv = jnp.dot(vt_ref[hh, j * (tk // tv)], pb[:tv], preferred_element_type=F32)
        for c in range(1, pb.shape[0] // tv):
            pv += jnp.dot(vt_ref[hh, j * (tk // tv) + c], pb[c * tv:(c + 1) * tv],
                          preferred_element_type=F32)
        acc_ref[hh, :, x * sw:(x + 1) * sw] = alpha * acc_ref[hh, :, x * sw:(x + 1) * sw] + pv

    def step(j, c, diag=None):
        live = [ci for ci, (hh, x) in enumerate(chains) if n_keys(x, diag) > 0]
        n = len(live)
        out = list(c)
        s_vals = {}
        sm_vals = {}
        for t in range(n + 2):
            if t < n:
                s_vals[t] = scores(live[t], j, diag)
            if 0 <= t - 1 < n:
                ci = live[t - 1]
                out[ci], alpha, pb = softmax(s_vals.pop(t - 1), c[ci])
                sm_vals[t - 1] = (alpha, pb)
            if 0 <= t - 2 < n:
                values(live[t - 2], j, *sm_vals.pop(t - 2))
        return tuple(out)

    c = lax.fori_loop(0, qi * kpq, lambda j, c: step(j, c), tuple(carry))
    for diag in range(kpq):
        c = step(qi * kpq + diag, c, diag)
    for hh, x in chains:
        acc = acc_ref[hh, :, x * sw:(x + 1) * sw]
        o_ref[x * sw:(x + 1) * sw, hh * V_HEAD:(hh + 1) * V_HEAD] = (
            acc[:V_HEAD] / acc[V_HEAD:V_HEAD + 1]).T.astype(o_ref.dtype)


def _attention(qt, k, vt, mk, mvt, tq):
    batch, n_heads, seq, _ = k.shape
    n_chunks, tv = vt.shape[2], vt.shape[4]
    mp = mk.shape[1]
    nh = ATTN_HEADS
    return pl.pallas_call(
        functools.partial(_attn_kernel, tq=tq, tv=tv),
        grid=(batch, n_heads // nh, seq // tq),
        in_specs=[
            pl.BlockSpec((None, nh, HEAD_PAD, tq), lambda b, h, i: (b, h, 0, i)),
            pl.BlockSpec((None, nh, seq, HEAD_PAD), lambda b, h, i: (b, h, 0, 0)),
            pl.BlockSpec((None, nh, n_chunks, V_AUG, tv), lambda b, h, i: (b, h, 0, 0, 0)),
            pl.BlockSpec((nh, mp, HEAD_PAD), lambda b, h, i: (h, 0, 0)),
            pl.BlockSpec((nh, V_AUG, mp), lambda b, h, i: (h, 0, 0)),
        ],
        out_specs=pl.BlockSpec((None, tq, nh * V_HEAD), lambda b, h, i: (b, i, h)),
        out_shape=jax.ShapeDtypeStruct((batch, seq, n_heads * V_HEAD), BF16),
        scratch_shapes=[pltpu.VMEM((nh, V_AUG, tq), F32)],
        compiler_params=_cparams(("arbitrary", "arbitrary", "arbitrary")),
        name="attention",
    )(qt, k, vt, mk, mvt)


def _out_proj_kernel(u_ref, halo_ref, mu_ref, y_ref, x_ref, wp_ref, ps_ref, wo_ref, o_ref,
                     *, tiles_per_batch, group):
    i = pl.program_id(1)
    first = (i % tiles_per_batch) == 0
    halo = jnp.where(first, mu_ref[...], halo_ref[...])
    ucat = jnp.concatenate([halo, u_ref[...]], axis=0)
    ys = []
    for gi, w in enumerate(POOL_WINDOWS):
        ug = ucat[:, gi * group:(gi + 1) * group]
        s = ug
        shift = 1
        while shift < w:
            s = s + pltpu.roll(s, shift, 0)
            shift *= 2
        diff = (s * (1.0 / w) - ug)[N_META:]
        yg = jnp.dot(diff.astype(BF16), wp_ref[gi], preferred_element_type=F32)
        ys.append((yg * ps_ref[:, gi * group:(gi + 1) * group]).astype(BF16))
    ycat = jnp.concatenate(ys + [y_ref[...]], axis=-1)
    o_ref[...] = x_ref[...] + jnp.dot(ycat, wo_ref[...], preferred_element_type=F32)


def _out_proj(u, mu, y_mla, x2d, w_pool, pool_scale, w_out, seq, tm, tn):
    n, d = x2d.shape
    pool_w = u.shape[1]
    mla_w = y_mla.shape[1]
    group = pool_w // len(POOL_WINDOWS)
    hb = tm // N_META
    return pl.pallas_call(
        functools.partial(_out_proj_kernel, tiles_per_batch=seq // tm, group=group),
        grid=(d // tn, n // tm),
        in_specs=[
            pl.BlockSpec((tm, pool_w), lambda j, i: (i, 0)),
            pl.BlockSpec((N_META, pool_w), lambda j, i: (jnp.maximum(i * hb - 1, 0), 0)),
            pl.BlockSpec((N_META, pool_w), lambda j, i: (0, 0)),
            pl.BlockSpec((tm, mla_w), lambda j, i: (i, 0)),
            pl.BlockSpec((tm, tn), lambda j, i: (i, j)),
            pl.BlockSpec(w_pool.shape, lambda j, i: (0, 0, 0)),
            pl.BlockSpec((1, pool_w), lambda j, i: (0, 0)),
            pl.BlockSpec((d, tn), lambda j, i: (0, j)),
        ],
        out_specs=pl.BlockSpec((tm, tn), lambda j, i: (i, j)),
        out_shape=jax.ShapeDtypeStruct((n, d), F32),
        compiler_params=_cparams(("arbitrary", "arbitrary")),
        name="out_proj",
    )(u, u, mu, y_mla, x2d, w_pool, pool_scale, w_out)


def _pack_bf16_pair(lo, hi):
    lo_b = lax.bitcast_convert_type(lo.astype(BF16).astype(F32), jnp.uint32)
    hi_b = lax.bitcast_convert_type(hi.astype(BF16).astype(F32), jnp.uint32)
    return lax.shift_right_logical(lo_b, jnp.uint32(16)) | hi_b


def _unpack_bf16_pair(w):
    lo = lax.bitcast_convert_type(lax.shift_left(w, jnp.uint32(16)), F32)
    hi = lax.bitcast_convert_type(w & jnp.uint32(0xFFFF0000), F32)
    return lo, hi


def _first_argmax(vals, sub, n):
    mx = jnp.max(vals, axis=0, keepdims=True)
    idx = jnp.min(jnp.where(vals == mx, sub, n), axis=0, keepdims=True)
    return mx, idx


def _router_kernel(h_ref, g_ref, wr_ref, br_ref, xp_ref, ri_ref, rf_ref, cnt_ref, *, d):
    xn = _rms(h_ref[...], g_ref[...], d)
    half = d // 2
    xp_ref[...] = _pack_bf16_pair(xn[:, :half], xn[:, half:])
    logits = lax.dot_general(wr_ref[...], xn.astype(BF16), (((1,), (1,)), ((), ())),
                             preferred_element_type=F32) + br_ref[...]
    tm = logits.shape[1]
    sub = lax.broadcasted_iota(jnp.int32, (N_GROUPS, tm), 0)
    lg = logits[:N_GROUPS]
    gmax, g_idx = _first_argmax(lg, sub, N_GROUPS)
    g_p = 1.0 / jnp.sum(jnp.exp(lg - gmax), axis=0, keepdims=True)
    in_group = jnp.zeros((EXPERTS_PER_GROUP, tm), F32)
    for g in range(N_GROUPS):
        lo = N_GROUPS + g * EXPERTS_PER_GROUP
        in_group = jnp.where(g_idx == g, logits[lo:lo + EXPERTS_PER_GROUP], in_group)
    m1, i1 = _first_argmax(in_group, sub, EXPERTS_PER_GROUP)
    rest = jnp.where(sub == i1, -jnp.inf, in_group)
    m2, i2 = _first_argmax(rest, sub, EXPERTS_PER_GROUP)
    e = jnp.exp(m2 - m1)
    w1 = g_p / (1.0 + e)
    w2 = g_p * e / (1.0 + e)
    e1 = g_idx * EXPERTS_PER_GROUP + i1
    e2 = g_idx * EXPERTS_PER_GROUP + i2
    ri_ref[...] = jnp.where(sub == 0, e1, jnp.where(sub == 1, e2, 0))
    rf_ref[...] = jnp.where(sub == 0, w1, jnp.where(sub == 1, w2, 0.0))
    sub_e = lax.broadcasted_iota(jnp.int32, (N_EXPERTS, tm), 0)
    hit = jnp.where((sub_e == e1) | (sub_e == e2), 1.0, 0.0)
    cnt = jnp.sum(hit, axis=1, keepdims=True)

    @pl.when(pl.program_id(0) == 0)
    def _():
        cnt_ref[...] = jnp.zeros_like(cnt_ref)

    cnt_ref[...] += jnp.broadcast_to(cnt, cnt_ref.shape)


def _router(h1, g, wr_t, br, tm):
    n, d = h1.shape
    return pl.pallas_call(
        functools.partial(_router_kernel, d=d),
        grid=(n // tm,),
        in_specs=[
            pl.BlockSpec((tm, d), lambda i: (i, 0)),
            pl.BlockSpec((1, d), lambda i: (0, 0)),
            pl.BlockSpec((ROUTER_ROWS, d), lambda i: (0, 0)),
            pl.BlockSpec((ROUTER_ROWS, 1), lambda i: (0, 0)),
        ],
        out_specs=[
            pl.BlockSpec((tm, d // 2), lambda i: (i, 0)),
            pl.BlockSpec((8, tm), lambda i: (0, i)),
            pl.BlockSpec((8, tm), lambda i: (0, i)),
            pl.BlockSpec((N_EXPERTS, 128), lambda i: (0, 0)),
        ],
        out_shape=[
            jax.ShapeDtypeStruct((n, d // 2), jnp.uint32),
            jax.ShapeDtypeStruct((8, n), jnp.int32),
            jax.ShapeDtypeStruct((8, n), F32),
            jax.ShapeDtypeStruct((N_EXPERTS, 128), F32),
        ],
        compiler_params=_cparams(("arbitrary",)),
        name="router",
    )(h1, g, wr_t, br)


def _dest_kernel(ri_ref, pstart_ref, tri_ref, d_ref, carry_ref):
    @pl.when(pl.program_id(0) == 0)
    def _():
        carry_ref[...] = jnp.zeros_like(carry_ref)

    tn = ri_ref.shape[1]
    e1 = ri_ref[0:1, :]
    e2 = ri_ref[1:2, :]
    sub = lax.broadcasted_iota(jnp.int32, (N_EXPERTS, tn), 0)
    oh1 = sub == e1
    oh2 = sub == e2
    oh = jnp.where(oh1 | oh2, 1.0, 0.0)
    c = jnp.dot(oh.astype(BF16), tri_ref[...], preferred_element_type=F32)
    val = c + (pstart_ref[...] + carry_ref[:, :1] - 1.0)
    d1 = jnp.sum(jnp.where(oh1, val, 0.0), axis=0, keepdims=True)
    d2 = jnp.sum(jnp.where(oh2, val, 0.0), axis=0, keepdims=True)
    sub8 = lax.broadcasted_iota(jnp.int32, (8, tn), 0)
    d_ref[...] = jnp.where(sub8 == 0, d1, jnp.where(sub8 == 1, d2, 0.0)).astype(jnp.int32)
    carry_ref[...] += jnp.broadcast_to(jnp.sum(oh, axis=1, keepdims=True), carry_ref.shape)


def _dest(ri, pstart_col, tn):
    n = ri.shape[1]
    tri = (jnp.arange(tn)[:, None] <= jnp.arange(tn)[None, :]).astype(BF16)
    return pl.pallas_call(
        _dest_kernel,
        grid=(n // tn,),
        in_specs=[
            pl.BlockSpec((8, tn), lambda i: (0, i)),
            pl.BlockSpec((N_EXPERTS, 1), lambda i: (0, 0)),
            pl.BlockSpec((tn, tn), lambda i: (0, 0)),
        ],
        out_specs=pl.BlockSpec((8, tn), lambda i: (0, i)),
        out_shape=jax.ShapeDtypeStruct((8, n), jnp.int32),
        scratch_shapes=[pltpu.VMEM((N_EXPERTS, 128), F32)],
        compiler_params=_cparams(("arbitrary",)),
        name="dest",
    )(ri, pstart_col, tri)


def _dispatch_kernel(d1_ref, d2_ref, cnt_ref, pstart_ref, padded_ref, npad_ref, nused_ref,
                     xp_ref, xs_ref, zero_ref, sem, zsem, bsem, *, tm, tb, n_blocks):
    i = pl.program_id(0)
    base = i * tm

    def row_copy(r, dst):
        return pltpu.make_async_copy(xp_ref.at[pl.ds(r, 1), :], xs_ref.at[pl.ds(dst, 1), :], sem)

    def zero_row_copy(dst):
        return pltpu.make_async_copy(zero_ref.at[pl.ds(0, 1), :], xs_ref.at[pl.ds(dst, 1), :], zsem)

    def zero_block_copy(blk):
        start = pl.multiple_of(blk * tb, tb)
        return pltpu.make_async_copy(zero_ref, xs_ref.at[pl.ds(start, tb), :], bsem)

    def issue(r, c):
        row_copy(r, d1_ref[base + r]).start()
        row_copy(r, d2_ref[base + r]).start()
        return c

    lax.fori_loop(0, tm, issue, 0, unroll=4)

    @pl.when(i == 0)
    def _():
        zero_ref[...] = jnp.zeros_like(zero_ref)

        def per_expert(e, c):
            lo = pstart_ref[e] + cnt_ref[e]
            hi = pstart_ref[e] + padded_ref[e]

            def per_row(p, c2):
                zero_row_copy(p).start()
                return c2

            lax.fori_loop(lo, hi, per_row, 0)
            return c

        lax.fori_loop(0, N_EXPERTS, per_expert, 0)

        def per_block(b, c):
            zero_block_copy(b).start()
            return c

        lax.fori_loop(nused_ref[0], n_blocks, per_block, 0)

        def zwait(p, c):
            zero_row_copy(0).wait()
            return c

        lax.fori_loop(0, npad_ref[0], zwait, 0)

        def bwait(b, c):
            zero_block_copy(0).wait()
            return c

        lax.fori_loop(nused_ref[0], n_blocks, bwait, 0)

    for _ in range(2):
        pltpu.make_async_copy(xp_ref, xs_ref.at[pl.ds(0, tm), :], sem).wait()


def _dispatch(d1, d2, cnt, pstart, padded, npad, n_used, xp, n_rows, tm, tb):
    n, half = xp.shape
    grid_spec = pltpu.PrefetchScalarGridSpec(
        num_scalar_prefetch=7,
        grid=(n // tm,),
        in_specs=[pl.BlockSpec((tm, half), lambda i, *_: (i, 0))],
        out_specs=pl.BlockSpec(memory_space=pl.ANY),
        scratch_shapes=[
            pltpu.VMEM((tb, half), jnp.uint32),
            pltpu.SemaphoreType.DMA,
            pltpu.SemaphoreType.DMA,
            pltpu.SemaphoreType.DMA,
        ],
    )
    return pl.pallas_call(
        functools.partial(_dispatch_kernel, tm=tm, tb=tb, n_blocks=n_rows // tb),
        grid_spec=grid_spec,
        out_shape=jax.ShapeDtypeStruct((n_rows, half), jnp.uint32),
        compiler_params=pltpu.CompilerParams(
            dimension_semantics=("arbitrary",), vmem_limit_bytes=VMEM_LIMIT, has_side_effects=True),
        name="dispatch",
    )(d1, d2, cnt, pstart, padded, npad, n_used, xp)


def _experts_kernel(be_ref, nused_ref, xs_ref, wg_ref, wu_ref, wd_hbm, ys_ref, wd_ref, wd_sem):
    i = pl.program_id(0)
    active = i < nused_ref[0]
    fresh = (i == 0) | (be_ref[i] != be_ref[jnp.maximum(i - 1, 0)])

    def wd_copy():
        return pltpu.make_async_copy(wd_hbm.at[be_ref[i]], wd_ref, wd_sem)

    @pl.when(active & fresh)
    def _():
        wd_copy().start()

    @pl.when(active)
    def _():
        lo, hi = _unpack_bf16_pair(xs_ref[...])
        half = lo.shape[1]
        g = (jnp.dot(lo, wg_ref[:half], preferred_element_type=F32)
             + jnp.dot(hi, wg_ref[half:], preferred_element_type=F32))
        u = (jnp.dot(lo, wu_ref[:half], preferred_element_type=F32)
             + jnp.dot(hi, wu_ref[half:], preferred_element_type=F32))
        hb = g / (1.0 + jnp.exp(-g)) * u

        @pl.when(fresh)
        def _():
            wd_copy().wait()

        y = jnp.dot(hb, wd_ref[...], preferred_element_type=F32)
        ys_ref[...] = _pack_bf16_pair(y[:, :half], y[:, half:])

    @pl.when(i >= nused_ref[0])
    def _():
        ys_ref[...] = jnp.zeros_like(ys_ref)


def _experts(blk_expert, n_used, xs, wg, wu, wd, tm):
    n_rows, half = xs.shape
    _, d, f = wg.shape
    n_blocks = n_rows // tm

    def row_map(i, be, nu):
        return (i, 0)

    def w_map(i, be, nu):
        return (be[i], 0, 0)

    grid_spec = pltpu.PrefetchScalarGridSpec(
        num_scalar_prefetch=2,
        grid=(n_blocks,),
        in_specs=[
            pl.BlockSpec((tm, half), row_map),
            pl.BlockSpec((None, d, f), w_map),
            pl.BlockSpec((None, d, f), w_map),
            pl.BlockSpec(memory_space=pl.ANY),
        ],
        out_specs=pl.BlockSpec((tm, half), row_map),
        scratch_shapes=[pltpu.VMEM((f, d), F32), pltpu.SemaphoreType.DMA],
    )
    return pl.pallas_call(
        _experts_kernel,
        grid_spec=grid_spec,
        out_shape=jax.ShapeDtypeStruct((n_rows, half), jnp.uint32),
        compiler_params=_cparams(("arbitrary",)),
        name="experts",
    )(blk_expert, n_used, xs, wg, wu, wd)


def _combine_kernel(d1_ref, d2_ref, h_ref, w_ref, ys_ref, o_ref, a_ref, b_ref, sem, *, tm, n_steps):
    i = pl.program_id(0)
    slot = i % 2

    def gather(step, s):
        base = step * tm

        def row_copy(src, r, buf):
            return pltpu.make_async_copy(ys_ref.at[pl.ds(src, 1), :], buf.at[s, pl.ds(r, 1), :], sem.at[s])

        def issue(r, c):
            row_copy(d1_ref[base + r], r, a_ref).start()
            row_copy(d2_ref[base + r], r, b_ref).start()
            return c

        lax.fori_loop(0, tm, issue, 0, unroll=4)

    @pl.when(i == 0)
    def _():
        gather(0, 0)

    @pl.when(i + 1 < n_steps)
    def _():
        gather(i + 1, 1 - slot)

    for buf in (a_ref, b_ref):
        pltpu.make_async_copy(ys_ref.at[pl.ds(0, tm), :], buf.at[slot], sem.at[slot]).wait()

    half = a_ref.shape[2]
    rows = 8

    def slab(r, c):
        r0 = pl.multiple_of(r * rows, rows)
        a_lo, a_hi = _unpack_bf16_pair(a_ref[slot, pl.ds(r0, rows), :])
        b_lo, b_hi = _unpack_bf16_pair(b_ref[slot, pl.ds(r0, rows), :])
        w1 = _lanes(w_ref[pl.ds(r0, rows), :128], half)
        w2 = _lanes(w_ref[pl.ds(r0, rows), 128:], half)
        o_ref[pl.ds(r0, rows), :half] = h_ref[pl.ds(r0, rows), :half] + (w1 * a_lo + w2 * b_lo)
        o_ref[pl.ds(r0, rows), half:] = h_ref[pl.ds(r0, rows), half:] + (w1 * a_hi + w2 * b_hi)
        return c

    lax.fori_loop(0, tm // rows, slab, 0, unroll=2)


def _combine(d1, d2, h1, wcol, ys, tm):
    n, d = h1.shape
    half = d // 2
    grid_spec = pltpu.PrefetchScalarGridSpec(
        num_scalar_prefetch=2,
        grid=(n // tm,),
        in_specs=[
            pl.BlockSpec((tm, d), lambda i, *_: (i, 0)),
            pl.BlockSpec((tm, 256), lambda i, *_: (i, 0)),
            pl.BlockSpec(memory_space=pl.ANY),
        ],
        out_specs=pl.BlockSpec((tm, d), lambda i, *_: (i, 0)),
        scratch_shapes=[
            pltpu.VMEM((2, tm, half), jnp.uint32),
            pltpu.VMEM((2, tm, half), jnp.uint32),
            pltpu.SemaphoreType.DMA((2,)),
        ],
    )
    return pl.pallas_call(
        functools.partial(_combine_kernel, tm=tm, n_steps=n // tm),
        grid_spec=grid_spec,
        out_shape=jax.ShapeDtypeStruct((n, d), F32),
        compiler_params=_cparams(("arbitrary",)),
        name="combine",
    )(d1, d2, h1, wcol, ys)


def _rope_tab(pos):
    inv = 1.0 / (ROPE_THETA ** (jnp.arange(0, QK_ROPE, 2, dtype=F32) / QK_ROPE))
    ang = pos.astype(F32)[:, None] * inv[None, :]
    c = jnp.cos(ang)
    s = jnp.sin(ang)
    cos64 = jnp.concatenate([c, c], axis=-1)
    sin64 = jnp.concatenate([-s, s], axis=-1)
    tab_q = jnp.concatenate([cos64, sin64], axis=-1)
    tab_k = jnp.concatenate([cos64, cos64, sin64, sin64], axis=-1)
    return tab_q, tab_k


def _swap_halves(a):
    h = a.shape[-1] // 2
    return jnp.concatenate([a[..., h:], a[..., :h]], axis=-1)


def _tile(n, pref):
    t = pref
    while n % t:
        t //= 2
    return t


def kernel(x, meta_tokens, mix_norm_g, w_in, q_lat_norm_g, w_uq, kv_lat_norm_g, w_ukv, q_head_norm_g,
           k_head_norm_g, w_pool, pool_scale, w_out, ffn_norm_g, w_group, b_group, w_expert, b_expert,
           w_gate, w_up, w_down):
    batch, seq, d = x.shape
    depth = w_in.shape[0]
    pool_w = pool_scale.shape[-1]
    q_rank = q_lat_norm_g.shape[-1]
    n_heads = w_uq.shape[-1] // QK_HEAD
    n = batch * seq
    f = w_gate.shape[-1]

    tab_q, tab_k = _rope_tab(N_META + jnp.arange(seq))
    tab_qt = tab_q.T
    meta_rows = 128
    _, mtab_k = _rope_tab(jnp.arange(meta_rows))
    tm_exp = _tile(n, 256)
    n_blocks = (2 * n + N_EXPERTS * (tm_exp - 1) + tm_exp - 1) // tm_exp
    n_rows = n_blocks * tm_exp

    h = x.reshape(n, d)
    meta = meta_tokens.astype(F32)
    for l in range(depth):
        k_r = w_in[l][:, pool_w + q_rank + KV_LORA:]
        k_rs = _swap_halves(k_r)
        w_in_p = jnp.concatenate([w_in[l][:, :pool_w + q_rank + KV_LORA], k_r, k_r, k_rs, k_rs],
                                 axis=-1).astype(BF16)
        wq = w_uq[l].reshape(q_rank, n_heads, QK_HEAD)
        wuq_p = jnp.concatenate([wq, _swap_halves(wq[..., QK_NOPE:])], axis=-1)
        wqt = wuq_p.reshape(q_rank, n_heads * HEAD_PAD).T.astype(BF16)
        wkv = w_ukv[l].reshape(KV_LORA, n_heads, QK_NOPE + V_HEAD)
        wk = wkv[..., :QK_NOPE].reshape(KV_LORA, n_heads * QK_NOPE).astype(BF16)
        wvt = wkv[..., QK_NOPE:].reshape(KV_LORA, n_heads * V_HEAD).T.astype(BF16)
        gq = q_head_norm_g[l]
        gk = k_head_norm_g[l]
        gqc = jnp.broadcast_to(jnp.concatenate([gq, _swap_halves(gq[QK_NOPE:])])[:, None], (HEAD_PAD, 128))
        gk_r = gk[QK_NOPE:]
        gkh = jnp.concatenate([gk[:QK_NOPE], gk_r, gk_r, _swap_halves(gk_r), _swap_halves(gk_r)])[None, :]
        wr_t = jnp.zeros((ROUTER_ROWS, d), F32)
        wr_t = wr_t.at[:N_GROUPS].set(w_group[l].T).at[N_GROUPS:N_GROUPS + N_EXPERTS].set(w_expert[l].T)
        br = jnp.zeros((ROUTER_ROWS, 1), F32)
        br = br.at[:N_GROUPS, 0].set(b_group[l]).at[N_GROUPS:N_GROUPS + N_EXPERTS, 0].set(b_expert[l])

        u, lat, mu, mlat = _in_proj(h, meta, mix_norm_g[l][None, :], w_in_p, pool_w, _tile(n, 256))
        mlat_p = jnp.pad(mlat, ((0, meta_rows - N_META), (0, 0)))
        qt, k, vt, mk, mvt = _qkv(lat, tab_k, tab_qt, mlat_p, mtab_k, q_lat_norm_g[l][None, :],
                                  kv_lat_norm_g[l][None, :], gqc, gkh, wqt, wk, wvt,
                                  batch, seq, n_heads, q_rank, _tile(seq, 256))
        y_mla = _attention(qt, k, vt, mk, mvt, _tile(seq, ATTN_TQ)).reshape(n, n_heads * V_HEAD)
        h1 = _out_proj(u, mu, y_mla, h, w_pool[l].astype(BF16), pool_scale[l][None, :],
                       w_out[l].astype(BF16), seq, _tile(seq, 256), _tile(d, 2048))

        xp, ri, rf, cnt = _router(h1, ffn_norm_g[l][None, :], wr_t.astype(BF16), br, _tile(n, 512))
        counts = cnt[:, 0].astype(jnp.int32)
        padded = (counts + tm_exp - 1) // tm_exp * tm_exp
        pend = jnp.cumsum(padded)
        pstart = pend - padded
        n_used = (pend[-1] // tm_exp).astype(jnp.int32)
        blk_start = jnp.arange(n_blocks, dtype=jnp.int32) * tm_exp
        blk_expert = jnp.minimum(jnp.searchsorted(pend, blk_start, side="right"), N_EXPERTS - 1)
        blk_expert = blk_expert.astype(jnp.int32)
        npad = (pend[-1] - 2 * n).astype(jnp.int32)

        dest = _dest(ri, pstart.astype(F32)[:, None], _tile(n, 512))
        d1 = dest[0]
        d2 = dest[1]
        xs = _dispatch(d1, d2, counts, pstart.astype(jnp.int32), padded.astype(jnp.int32),
                       npad[None], n_used[None], xp, n_rows, _tile(n, 1024), tm_exp)
        ys = _experts(blk_expert, n_used[None], xs, w_gate[l], w_up[l], w_down[l], tm_exp)
        wcol = jnp.broadcast_to(rf[:2].T[:, :, None], (n, 2, 128)).reshape(n, 256)
        h = _combine(d1, d2, h1, wcol, ys, _tile(n, 256))
        if l + 1 < depth:
            raise NotImplementedError("meta-token outputs are only produced for a single layer")
    return h.reshape(batch, seq, d)
```

```python
import functools

import jax
import jax.numpy as jnp
from jax import lax
from jax.experimental import pallas as pl
from jax.experimental.pallas import tpu as pltpu

F32 = jnp.float32
BF16 = jnp.bfloat16

N_META = 16
POOL_WINDOWS = (2, 4, 8, 16)
QK_NOPE = 128
QK_ROPE = 64
QK_HEAD = QK_NOPE + QK_ROPE
V_HEAD = 128
V_AUG = V_HEAD + 16
LOG2E = 1.4426950408889634
HEAD_PAD = 256
KV_LORA = 512
ROPE_THETA = 10000.0
N_GROUPS = 8
EXPERTS_PER_GROUP = 8
N_EXPERTS = N_GROUPS * EXPERTS_PER_GROUP
EPS = 1e-6
ROUTER_ROWS = 128
NEG_BIG = -1e30

VMEM_LIMIT = 56 * 1024 * 1024


def _cparams(sem, vmem=VMEM_LIMIT):
    return pltpu.CompilerParams(dimension_semantics=sem, vmem_limit_bytes=vmem)


def _rms(xf, g, n):
    ss = jnp.sum(xf * xf, axis=-1, keepdims=True)
    return xf * lax.rsqrt(ss * (1.0 / n) + EPS) * g


def _in_proj_kernel(x_ref, meta_ref, g_ref, w_ref, u_ref, lat_ref, mu_ref, mlat_ref, *, n_u, d):
    def proj(xf):
        xn = _rms(xf, g_ref[...], d).astype(BF16)
        return jnp.dot(xn, w_ref[...], preferred_element_type=F32)

    p = proj(x_ref[...])
    u_ref[...] = p[:, :n_u]
    lat_ref[...] = p[:, n_u:]

    @pl.when(pl.program_id(0) == 0)
    def _():
        pm = proj(meta_ref[...])
        mu_ref[...] = pm[:, :n_u]
        mlat_ref[...] = pm[:, n_u:]


def _in_proj(x2d, meta, g, w_p, n_u, tm):
    n, d = x2d.shape
    n_out = w_p.shape[1]
    n_lat = n_out - n_u
    return pl.pallas_call(
        functools.partial(_in_proj_kernel, n_u=n_u, d=d),
        grid=(n // tm,),
        in_specs=[
            pl.BlockSpec((tm, d), lambda i: (i, 0)),
            pl.BlockSpec((N_META, d), lambda i: (0, 0)),
            pl.BlockSpec((1, d), lambda i: (0, 0)),
            pl.BlockSpec((d, n_out), lambda i: (0, 0), pipeline_mode=pl.Buffered(1)),
        ],
        out_specs=[
            pl.BlockSpec((tm, n_u), lambda i: (i, 0)),
            pl.BlockSpec((tm, n_lat), lambda i: (i, 0)),
            pl.BlockSpec((N_META, n_u), lambda i: (0, 0)),
            pl.BlockSpec((N_META, n_lat), lambda i: (0, 0)),
        ],
        out_shape=[
            jax.ShapeDtypeStruct((n, n_u), F32),
            jax.ShapeDtypeStruct((n, n_lat), F32),
            jax.ShapeDtypeStruct((N_META, n_u), F32),
            jax.ShapeDtypeStruct((N_META, n_lat), F32),
        ],
        compiler_params=_cparams(("arbitrary",)),
        name="in_proj",
    )(x2d, meta, g, w_p)


def _lanes(a, m):
    return jnp.concatenate([a] * (m // 128), axis=1) if m > 128 else a


def _ones_rows(r, m):
    return jnp.where(lax.broadcasted_iota(jnp.int32, (r, m), 0) == 0, 1.0, 0.0).astype(BF16)


def _kv_rows(lat, tab, gkvl_ref, gkh_ref, wk_ref, wvt_ref, store_k, store_vt, *, n_heads, q_rank):
    kvl = lat[:, q_rank:q_rank + KV_LORA]
    kd = lat[:, q_rank + KV_LORA:q_rank + KV_LORA + 128]
    kr = lat[:, q_rank + KV_LORA + 128:q_rank + KV_LORA + 256]
    kvn = _rms(kvl, gkvl_ref[...], KV_LORA)
    kvn_b = kvn.astype(BF16)
    kvn_t = kvn.T.astype(BF16)
    gk0 = gkh_ref[:, :128]
    k_rope = kd * gkh_ref[:, 128:256] * tab[:, :128] + kr * gkh_ref[:, 256:384] * tab[:, 128:256]
    ss_kr = 0.5 * jnp.sum(kd * kd, axis=-1, keepdims=True)
    inv_n = 1.0 / QK_HEAD
    for p in range(n_heads // 2):
        kk = jnp.dot(kvn_b, wk_ref[:, p * 256:(p + 1) * 256], preferred_element_type=F32)
        vv = jnp.dot(wvt_ref[p * 256:(p + 1) * 256, :], kvn_t, preferred_element_type=F32)
        for t in range(2):
            kn = kk[:, t * 128:(t + 1) * 128]
            ssk = jnp.sum(kn * kn, axis=-1, keepdims=True) + ss_kr
            rk = lax.rsqrt(ssk * inv_n + EPS)
            store_k(2 * p + t, jnp.concatenate([kn * rk * gk0, k_rope * rk], axis=-1).astype(BF16))
            store_vt(2 * p + t, vv[t * 128:(t + 1) * 128].astype(BF16))


def _qkv_kernel(lat_ref, tab_ref, tqt_ref, mlat_ref, mtab_ref, gql_ref, gkvl_ref, gqc_ref, gkh_ref,
                wqt_ref, wk_ref, wvt_ref, qt_ref, k_ref, vt_ref, mk_ref, mvt_ref, *, n_heads, q_rank):
    lat = lat_ref[...]
    m = lat.shape[0]

    def sk(h, val):
        k_ref[h] = val

    def svt(h, val):
        vt_ref[h, :V_HEAD, :] = val
        vt_ref[h, V_HEAD:, :] = _ones_rows(V_AUG - V_HEAD, val.shape[1])

    _kv_rows(lat, tab_ref[...], gkvl_ref, gkh_ref, wk_ref, wvt_ref, sk, svt, n_heads=n_heads, q_rank=q_rank)

    scale = QK_HEAD ** -0.5 * LOG2E
    qn_t = _rms(lat[:, :q_rank], gql_ref[...], q_rank).T.astype(BF16)
    gq0 = _lanes(gqc_ref[:128, :], m) * scale
    gq1 = _lanes(gqc_ref[128:, :], m) * tqt_ref[...] * scale
    inv_n = 1.0 / QK_HEAD
    for h in range(n_heads):
        xh = jnp.dot(wqt_ref[h * HEAD_PAD:(h + 1) * HEAD_PAD, :], qn_t, preferred_element_type=F32)
        x0 = xh[:128]
        x1 = xh[128:]
        ss = jnp.sum(x0 * x0, axis=0, keepdims=True) + 0.5 * jnp.sum(x1 * x1, axis=0, keepdims=True)
        rinv = lax.rsqrt(ss * inv_n + EPS)
        qt_ref[h] = jnp.concatenate([x0 * rinv * gq0, x1 * rinv * gq1], axis=0).astype(BF16)

    @pl.when((pl.program_id(0) == 0) & (pl.program_id(1) == 0))
    def _():
        def smk(h, val):
            mk_ref[h] = val

        def smvt(h, val):
            mvt_ref[h, :V_HEAD, :] = val
            mvt_ref[h, V_HEAD:, :] = _ones_rows(V_AUG - V_HEAD, val.shape[1])

        _kv_rows(mlat_ref[...], mtab_ref[...], gkvl_ref, gkh_ref, wk_ref, wvt_ref, smk, smvt,
                 n_heads=n_heads, q_rank=q_rank)


def _qkv(lat, tab, tqt, mlat, mtab, gql, gkvl, gqc, gkh, wqt, wk, wvt, batch, seq, n_heads, q_rank, tm):
    n_lat = lat.shape[1]
    tpb = seq // tm
    mp = mlat.shape[0]
    const = lambda b, i: (0, 0)
    return pl.pallas_call(
        functools.partial(_qkv_kernel, n_heads=n_heads, q_rank=q_rank),
        grid=(batch, tpb),
        in_specs=[
            pl.BlockSpec((tm, n_lat), lambda b, i: (b * tpb + i, 0)),
            pl.BlockSpec((tm, 256), lambda b, i: (i, 0)),
            pl.BlockSpec((128, tm), lambda b, i: (0, i)),
            pl.BlockSpec((mp, n_lat), const),
            pl.BlockSpec((mp, 256), const),
            pl.BlockSpec((1, q_rank), const),
            pl.BlockSpec((1, KV_LORA), const),
            pl.BlockSpec((256, 128), const),
            pl.BlockSpec((1, 384), const),
            pl.BlockSpec(wqt.shape, const, pipeline_mode=pl.Buffered(1)),
            pl.BlockSpec(wk.shape, const, pipeline_mode=pl.Buffered(1)),
            pl.BlockSpec(wvt.shape, const, pipeline_mode=pl.Buffered(1)),
        ],
        out_specs=[
            pl.BlockSpec((None, n_heads, HEAD_PAD, tm), lambda b, i: (b, 0, 0, i)),
            pl.BlockSpec((None, n_heads, tm, HEAD_PAD), lambda b, i: (b, 0, i, 0)),
            pl.BlockSpec((None, n_heads, None, V_AUG, tm), lambda b, i: (b, 0, i, 0, 0)),
            pl.BlockSpec((n_heads, mp, HEAD_PAD), lambda b, i: (0, 0, 0)),
            pl.BlockSpec((n_heads, V_AUG, mp), lambda b, i: (0, 0, 0)),
        ],
        out_shape=[
            jax.ShapeDtypeStruct((batch, n_heads, HEAD_PAD, seq), BF16),
            jax.ShapeDtypeStruct((batch, n_heads, seq, HEAD_PAD), BF16),
            jax.ShapeDtypeStruct((batch, n_heads, tpb, V_AUG, tm), BF16),
            jax.ShapeDtypeStruct((n_heads, mp, HEAD_PAD), BF16),
            jax.ShapeDtypeStruct((n_heads, V_AUG, mp), BF16),
        ],
        compiler_params=_cparams(("arbitrary", "arbitrary")),
        name="qkv",
    )(lat, tab, tqt, mlat, mtab, gql, gkvl, gqc, gkh, wqt, wk, wvt)


ATTN_HEADS = 2
ATTN_STRIP = 256
ATTN_TQ = 2048
ATTN_TK = 1024
ATTN_SM_LAG = 1
ATTN_PV_LAG = 3


def _attn_kernel(qt_ref, k_ref, vt_ref, mk_ref, mvt_ref, o_ref, acc_ref, *, tq, tv):
    qi = pl.program_id(2)
    nh = qt_ref.shape[0]
    sw = ATTN_STRIP
    chains = [(hh, x) for x in range(tq // sw) for hh in range(nh)]
    qts = [qt_ref[hh, :, x * sw:(x + 1) * sw] for hh, x in chains]

    carry = []
    for ci, (hh, x) in enumerate(chains):
        s0 = jnp.dot(mk_ref[hh], qts[ci], preferred_element_type=F32)
        s0 = jnp.where(lax.broadcasted_iota(jnp.int32, s0.shape, 0) < N_META, s0, NEG_BIG)
        m0 = jnp.max(s0, axis=0, keepdims=True)
        carry.append(m0)
        acc_ref[hh, :, x * sw:(x + 1) * sw] = jnp.dot(mvt_ref[hh], jnp.exp2(s0 - m0).astype(BF16),
                                                     preferred_element_type=F32)

    tk = ATTN_TK
    kpq = tq // tk

    def n_keys(x, diag):
        if diag is None:
            return tk
        return max(0, min(tk, (x + 1) * sw - diag * tk))

    def scores(ci, j, diag):
        hh, x = chains[ci]
        start = pl.multiple_of(j * tk, tk)
        s = jnp.dot(k_ref[hh, pl.ds(start, n_keys(x, diag)), :], qts[ci],
                    preferred_element_type=F32)
        if diag is not None and (diag + 1) * tk - 1 > x * sw:
            key = lax.broadcasted_iota(jnp.int32, s.shape, 0) + diag * tk
            qry = lax.broadcasted_iota(jnp.int32, s.shape, 1) + x * sw
            s = jnp.where(key <= qry, s, NEG_BIG)
        return s

    def softmax(s, m_prev):
        m_new = jnp.maximum(m_prev, jnp.max(s, axis=0, keepdims=True))
        return m_new, jnp.exp2(m_prev - m_new), jnp.exp2((s - m_new).astype(BF16))

    def values(ci, j, alpha, pb):
        hh, x = chains[ci]
        pv = jnp.dot(vt_ref[hh, j * (tk // tv)], pb[:tv], preferred_element_type=F32)
        for c in range(1, pb.shape[0] // tv):
            pv += jnp.dot(vt_ref[hh, j * (tk // tv) + c], pb[c * tv:(c + 1) * tv],
                          preferred_element_type=F32)
        acc_ref[hh, :, x * sw:(x + 1) * sw] = alpha * acc_ref[hh, :, x * sw:(x + 1) * sw] + pv

    def step(j, c, diag=None):
        live = [ci for ci, (hh, x) in enumerate(chains) if n_keys(x, diag) > 0]
        n = len(live)
        out = list(c)
        s_vals = {}
        sm_vals = {}
        lag = ATTN_PV_LAG
        for t in range(n + lag):
            if t < n:
                s_vals[t] = scores(live[t], j, diag)
            if 0 <= t - ATTN_SM_LAG < n:
                ts = t - ATTN_SM_LAG
                ci = live[ts]
                out[ci], alpha, pb = softmax(s_vals.pop(ts), c[ci])
                sm_vals[ts] = (alpha, pb)
            if 0 <= t - lag < n:
                values(live[t - lag], j, *sm_vals.pop(t - lag))
        return tuple(out)

    c = lax.fori_loop(0, qi * kpq, lambda j, c: step(j, c), tuple(carry))
    for diag in range(kpq):
        c = step(qi * kpq + diag, c, diag)
    for hh, x in chains:
        acc = acc_ref[hh, :, x * sw:(x + 1) * sw]
        o_ref[x * sw:(x + 1) * sw, hh * V_HEAD:(hh + 1) * V_HEAD] = (
            acc[:V_HEAD] / acc[V_HEAD:V_HEAD + 1]).T.astype(o_ref.dtype)


def _attention(qt, k, vt, mk, mvt, tq):
    batch, n_heads, seq, _ = k.shape
    n_chunks, tv = vt.shape[2], vt.shape[4]
    mp = mk.shape[1]
    nh = ATTN_HEADS
    return pl.pallas_call(
        functools.partial(_attn_kernel, tq=tq, tv=tv),
        grid=(batch, n_heads // nh, seq // tq),
        in_specs=[
            pl.BlockSpec((None, nh, HEAD_PAD, tq), lambda b, h, i: (b, h, 0, i)),
            pl.BlockSpec((None, nh, seq, HEAD_PAD), lambda b, h, i: (b, h, 0, 0)),
            pl.BlockSpec((None, nh, n_chunks, V_AUG, tv), lambda b, h, i: (b, h, 0, 0, 0)),
            pl.BlockSpec((nh, mp, HEAD_PAD), lambda b, h, i: (h, 0, 0)),
            pl.BlockSpec((nh, V_AUG, mp), lambda b, h, i: (h, 0, 0)),
        ],
        out_specs=pl.BlockSpec((None, tq, nh * V_HEAD), lambda b, h, i: (b, i, h)),
        out_shape=jax.ShapeDtypeStruct((batch, seq, n_heads * V_HEAD), BF16),
        scratch_shapes=[pltpu.VMEM((nh, V_AUG, tq), F32)],
        compiler_params=_cparams(("arbitrary", "arbitrary", "arbitrary")),
        name="attention",
    )(qt, k, vt, mk, mvt)


def _out_proj_kernel(u_ref, halo_ref, mu_ref, y_ref, x_ref, wp_ref, ps_ref, wo_ref, o_ref,
                     *, tiles_per_batch, group):
    i = pl.program_id(1)
    first = (i % tiles_per_batch) == 0
    halo = jnp.where(first, mu_ref[...], halo_ref[...])
    ucat = jnp.concatenate([halo, u_ref[...]], axis=0)
    ys = []
    for gi, w in enumerate(POOL_WINDOWS):
        ug = ucat[:, gi * group:(gi + 1) * group]
        s = ug
        shift = 1
        while shift < w:
            s = s + pltpu.roll(s, shift, 0)
            shift *= 2
        diff = (s * (1.0 / w) - ug)[N_META:]
        yg = jnp.dot(diff.astype(BF16), wp_ref[gi], preferred_element_type=F32)
        ys.append((yg * ps_ref[:, gi * group:(gi + 1) * group]).astype(BF16))
    ycat = jnp.concatenate(ys + [y_ref[...]], axis=-1)
    o_ref[...] = x_ref[...] + jnp.dot(ycat, wo_ref[...], preferred_element_type=F32)


def _out_proj(u, mu, y_mla, x2d, w_pool, pool_scale, w_out, seq, tm, tn):
    n, d = x2d.shape
    pool_w = u.shape[1]
    mla_w = y_mla.shape[1]
    group = pool_w // len(POOL_WINDOWS)
    hb = tm // N_META
    return pl.pallas_call(
        functools.partial(_out_proj_kernel, tiles_per_batch=seq // tm, group=group),
        grid=(d // tn, n // tm),
        in_specs=[
            pl.BlockSpec((tm, pool_w), lambda j, i: (i, 0)),
            pl.BlockSpec((N_META, pool_w), lambda j, i: (jnp.maximum(i * hb - 1, 0), 0)),
            pl.BlockSpec((N_META, pool_w), lambda j, i: (0, 0)),
            pl.BlockSpec((tm, mla_w), lambda j, i: (i, 0)),
            pl.BlockSpec((tm, tn), lambda j, i: (i, j)),
            pl.BlockSpec(w_pool.shape, lambda j, i: (0, 0, 0)),
            pl.BlockSpec((1, pool_w), lambda j, i: (0, 0)),
            pl.BlockSpec((d, tn), lambda j, i: (0, j)),
        ],
        out_specs=pl.BlockSpec((tm, tn), lambda j, i: (i, j)),
        out_shape=jax.ShapeDtypeStruct((n, d), F32),
        compiler_params=_cparams(("arbitrary", "arbitrary")),
        name="out_proj",
    )(u, u, mu, y_mla, x2d, w_pool, pool_scale, w_out)


def _pack_bf16_pair(lo, hi):
    lo_b = lax.bitcast_convert_type(lo.astype(BF16).astype(F32), jnp.uint32)
    hi_b = lax.bitcast_convert_type(hi.astype(BF16).astype(F32), jnp.uint32)
    return lax.shift_right_logical(lo_b, jnp.uint32(16)) | hi_b


def _unpack_bf16_pair(w):
    lo = lax.bitcast_convert_type(lax.shift_left(w, jnp.uint32(16)), F32)
    hi = lax.bitcast_convert_type(w & jnp.uint32(0xFFFF0000), F32)
    return lo, hi


def _first_argmax(vals, sub, n):
    mx = jnp.max(vals, axis=0, keepdims=True)
    idx = jnp.min(jnp.where(vals == mx, sub, n), axis=0, keepdims=True)
    return mx, idx


def _router_kernel(h_ref, g_ref, wr_ref, br_ref, xp_ref, ri_ref, rf_ref, cnt_ref, *, d):
    xn = _rms(h_ref[...], g_ref[...], d)
    half = d // 2
    xp_ref[...] = _pack_bf16_pair(xn[:, :half], xn[:, half:])
    logits = lax.dot_general(wr_ref[...], xn.astype(BF16), (((1,), (1,)), ((), ())),
                             preferred_element_type=F32) + br_ref[...]
    tm = logits.shape[1]
    sub = lax.broadcasted_iota(jnp.int32, (N_GROUPS, tm), 0)
    lg = logits[:N_GROUPS]
    gmax, g_idx = _first_argmax(lg, sub, N_GROUPS)
    g_p = 1.0 / jnp.sum(jnp.exp(lg - gmax), axis=0, keepdims=True)
    in_group = jnp.zeros((EXPERTS_PER_GROUP, tm), F32)
    for g in range(N_GROUPS):
        lo = N_GROUPS + g * EXPERTS_PER_GROUP
        in_group = jnp.where(g_idx == g, logits[lo:lo + EXPERTS_PER_GROUP], in_group)
    m1, i1 = _first_argmax(in_group, sub, EXPERTS_PER_GROUP)
    rest = jnp.where(sub == i1, -jnp.inf, in_group)
    m2, i2 = _first_argmax(rest, sub, EXPERTS_PER_GROUP)
    e = jnp.exp(m2 - m1)
    w1 = g_p / (1.0 + e)
    w2 = g_p * e / (1.0 + e)
    e1 = g_idx * EXPERTS_PER_GROUP + i1
    e2 = g_idx * EXPERTS_PER_GROUP + i2
    ri_ref[...] = jnp.where(sub == 0, e1, jnp.where(sub == 1, e2, 0))
    rf_ref[:, :128] = jnp.broadcast_to(w1, (128, tm)).T
    rf_ref[:, 128:] = jnp.broadcast_to(w2, (128, tm)).T
    sub_e = lax.broadcasted_iota(jnp.int32, (N_EXPERTS, tm), 0)
    hit = jnp.where((sub_e == e1) | (sub_e == e2), 1.0, 0.0)
    cnt = jnp.sum(hit, axis=1, keepdims=True)

    @pl.when(pl.program_id(0) == 0)
    def _():
        cnt_ref[...] = jnp.zeros_like(cnt_ref)

    cnt_ref[...] += jnp.broadcast_to(cnt, cnt_ref.shape)


def _router(h1, g, wr_t, br, tm):
    n, d = h1.shape
    return pl.pallas_call(
        functools.partial(_router_kernel, d=d),
        grid=(n // tm,),
        in_specs=[
            pl.BlockSpec((tm, d), lambda i: (i, 0)),
            pl.BlockSpec((1, d), lambda i: (0, 0)),
            pl.BlockSpec((ROUTER_ROWS, d), lambda i: (0, 0)),
            pl.BlockSpec((ROUTER_ROWS, 1), lambda i: (0, 0)),
        ],
        out_specs=[
            pl.BlockSpec((tm, d // 2), lambda i: (i, 0)),
            pl.BlockSpec((8, tm), lambda i: (0, i)),
            pl.BlockSpec((tm, 256), lambda i: (i, 0)),
            pl.BlockSpec((N_EXPERTS, 128), lambda i: (0, 0)),
        ],
        out_shape=[
            jax.ShapeDtypeStruct((n, d // 2), jnp.uint32),
            jax.ShapeDtypeStruct((8, n), jnp.int32),
            jax.ShapeDtypeStruct((n, 256), F32),
            jax.ShapeDtypeStruct((N_EXPERTS, 128), F32),
        ],
        compiler_params=_cparams(("arbitrary",)),
        name="router",
    )(h1, g, wr_t, br)


def _dest_kernel(ri_ref, pstart_ref, tri_ref, d_ref, carry_ref):
    @pl.when(pl.program_id(0) == 0)
    def _():
        carry_ref[...] = jnp.zeros_like(carry_ref)

    tn = ri_ref.shape[1]
    e1 = ri_ref[0:1, :]
    e2 = ri_ref[1:2, :]
    sub = lax.broadcasted_iota(jnp.int32, (N_EXPERTS, tn), 0)
    oh1 = sub == e1
    oh2 = sub == e2
    oh = jnp.where(oh1 | oh2, 1.0, 0.0)
    c = jnp.dot(oh.astype(BF16), tri_ref[...], preferred_element_type=F32)
    val = c + (pstart_ref[...] + carry_ref[:, :1] - 1.0)
    d1 = jnp.sum(jnp.where(oh1, val, 0.0), axis=0, keepdims=True)
    d2 = jnp.sum(jnp.where(oh2, val, 0.0), axis=0, keepdims=True)
    sub8 = lax.broadcasted_iota(jnp.int32, (8, tn), 0)
    d_ref[...] = jnp.where(sub8 == 0, d1, jnp.where(sub8 == 1, d2, 0.0)).astype(jnp.int32)
    carry_ref[...] += jnp.broadcast_to(jnp.sum(oh, axis=1, keepdims=True), carry_ref.shape)


def _dest(ri, pstart_col, tn):
    n = ri.shape[1]
    tri = (jnp.arange(tn)[:, None] <= jnp.arange(tn)[None, :]).astype(BF16)
    return pl.pallas_call(
        _dest_kernel,
        grid=(n // tn,),
        in_specs=[
            pl.BlockSpec((8, tn), lambda i: (0, i)),
            pl.BlockSpec((N_EXPERTS, 1), lambda i: (0, 0)),
            pl.BlockSpec((tn, tn), lambda i: (0, 0)),
        ],
        out_specs=pl.BlockSpec((8, tn), lambda i: (0, i)),
        out_shape=jax.ShapeDtypeStruct((8, n), jnp.int32),
        scratch_shapes=[pltpu.VMEM((N_EXPERTS, 128), F32)],
        compiler_params=_cparams(("arbitrary",)),
        name="dest",
    )(ri, pstart_col, tri)


def _dispatch_kernel(d1_ref, d2_ref, cnt_ref, pstart_ref, padded_ref, npad_ref, nused_ref,
                     xp_ref, xs_ref, zero_ref, sem, zsem, bsem, *, tm, tb, n_blocks):
    i = pl.program_id(0)
    base = i * tm

    def row_copy(r, dst):
        return pltpu.make_async_copy(xp_ref.at[pl.ds(r, 1), :], xs_ref.at[pl.ds(dst, 1), :], sem)

    def zero_row_copy(dst):
        return pltpu.make_async_copy(zero_ref.at[pl.ds(0, 1), :], xs_ref.at[pl.ds(dst, 1), :], zsem)

    def zero_block_copy(blk):
        start = pl.multiple_of(blk * tb, tb)
        return pltpu.make_async_copy(zero_ref, xs_ref.at[pl.ds(start, tb), :], bsem)

    def issue(r, c):
        row_copy(r, d1_ref[base + r]).start()
        row_copy(r, d2_ref[base + r]).start()
        return c

    lax.fori_loop(0, tm, issue, 0, unroll=4)

    @pl.when(i == 0)
    def _():
        zero_ref[...] = jnp.zeros_like(zero_ref)

        def per_expert(e, c):
            lo = pstart_ref[e] + cnt_ref[e]
            hi = pstart_ref[e] + padded_ref[e]

            def per_row(p, c2):
                zero_row_copy(p).start()
                return c2

            lax.fori_loop(lo, hi, per_row, 0)
            return c

        lax.fori_loop(0, N_EXPERTS, per_expert, 0)

        def per_block(b, c):
            zero_block_copy(b).start()
            return c

        lax.fori_loop(nused_ref[0], n_blocks, per_block, 0)

        def zwait(p, c):
            zero_row_copy(0).wait()
            return c

        lax.fori_loop(0, npad_ref[0], zwait, 0)

        def bwait(b, c):
            zero_block_copy(0).wait()
            return c

        lax.fori_loop(nused_ref[0], n_blocks, bwait, 0)

    for _ in range(2):
        pltpu.make_async_copy(xp_ref, xs_ref.at[pl.ds(0, tm), :], sem).wait()


def _dispatch(d1, d2, cnt, pstart, padded, npad, n_used, xp, n_rows, tm, tb):
    n, half = xp.shape
    grid_spec = pltpu.PrefetchScalarGridSpec(
        num_scalar_prefetch=7,
        grid=(n // tm,),
        in_specs=[pl.BlockSpec((tm, half), lambda i, *_: (i, 0))],
        out_specs=pl.BlockSpec(memory_space=pl.ANY),
        scratch_shapes=[
            pltpu.VMEM((tb, half), jnp.uint32),
            pltpu.SemaphoreType.DMA,
            pltpu.SemaphoreType.DMA,
            pltpu.SemaphoreType.DMA,
        ],
    )
    return pl.pallas_call(
        functools.partial(_dispatch_kernel, tm=tm, tb=tb, n_blocks=n_rows // tb),
        grid_spec=grid_spec,
        out_shape=jax.ShapeDtypeStruct((n_rows, half), jnp.uint32),
        compiler_params=pltpu.CompilerParams(
            dimension_semantics=("arbitrary",), vmem_limit_bytes=VMEM_LIMIT, has_side_effects=True),
        name="dispatch",
    )(d1, d2, cnt, pstart, padded, npad, n_used, xp)


def _experts_kernel(be_ref, nused_ref, slot_ref, nexte_ref, xs_ref, wg_hbm, wu_hbm, wd_hbm, ys_ref,
                    wg_ref, wu_ref, wd_ref, gu_sem, wd_sem):
    i = pl.program_id(0)
    active = i < nused_ref[0]
    fresh = (i == 0) | (be_ref[i] != be_ref[jnp.maximum(i - 1, 0)])
    slot = slot_ref[i]

    def gu_copies(e, s):
        return (pltpu.make_async_copy(wg_hbm.at[e], wg_ref.at[s], gu_sem.at[s]),
                pltpu.make_async_copy(wu_hbm.at[e], wu_ref.at[s], gu_sem.at[s]))

    def wd_copy():
        return pltpu.make_async_copy(wd_hbm.at[be_ref[i]], wd_ref, wd_sem)

    @pl.when(active & (i == 0))
    def _():
        for cp in gu_copies(be_ref[0], 0):
            cp.start()

    @pl.when(active & fresh)
    def _():
        wd_copy().start()
        for cp in gu_copies(be_ref[i], slot):
            cp.wait()

        @pl.when(nexte_ref[i] >= 0)
        def _():
            for cp in gu_copies(nexte_ref[i], 1 - slot):
                cp.start()

    @pl.when(active)
    def _():
        lo, hi = _unpack_bf16_pair(xs_ref[...])
        half = lo.shape[1]
        g = (jnp.dot(lo, wg_ref[slot, :half], preferred_element_type=F32)
             + jnp.dot(hi, wg_ref[slot, half:], preferred_element_type=F32))
        u = (jnp.dot(lo, wu_ref[slot, :half], preferred_element_type=F32)
             + jnp.dot(hi, wu_ref[slot, half:], preferred_element_type=F32))
        hb = g / (1.0 + jnp.exp(-g)) * u

        @pl.when(fresh)
        def _():
            wd_copy().wait()

        y = jnp.dot(hb, wd_ref[...], preferred_element_type=F32)
        ys_ref[...] = _pack_bf16_pair(y[:, :half], y[:, half:])

    @pl.when(i >= nused_ref[0])
    def _():
        ys_ref[...] = jnp.zeros_like(ys_ref)


def _experts(blk_expert, n_used, xs, wg, wu, wd, tm):
    n_rows, half = xs.shape
    _, d, f = wg.shape
    n_blocks = n_rows // tm

    idx = jnp.arange(n_blocks, dtype=jnp.int32)
    fresh = jnp.concatenate([jnp.ones((1,), bool), blk_expert[1:] != blk_expert[:-1]]) & (idx < n_used[0])
    slot = ((jnp.cumsum(fresh.astype(jnp.int32)) - 1) % 2).astype(jnp.int32)
    later = jnp.where(fresh, idx, n_blocks)
    nxt = lax.cummin(jnp.concatenate([later[1:], jnp.full((1,), n_blocks, jnp.int32)]), reverse=True)
    next_e = jnp.where(nxt < n_blocks, blk_expert[jnp.minimum(nxt, n_blocks - 1)], -1).astype(jnp.int32)

    def row_map(i, *_):
        return (i, 0)

    grid_spec = pltpu.PrefetchScalarGridSpec(
        num_scalar_prefetch=4,
        grid=(n_blocks,),
        in_specs=[
            pl.BlockSpec((tm, half), row_map),
            pl.BlockSpec(memory_space=pl.ANY),
            pl.BlockSpec(memory_space=pl.ANY),
            pl.BlockSpec(memory_space=pl.ANY),
        ],
        out_specs=pl.BlockSpec((tm, half), row_map),
        scratch_shapes=[
            pltpu.VMEM((2, d, f), F32),
            pltpu.VMEM((2, d, f), F32),
            pltpu.VMEM((f, d), F32),
            pltpu.SemaphoreType.DMA((2,)),
            pltpu.SemaphoreType.DMA,
        ],
    )
    return pl.pallas_call(
        _experts_kernel,
        grid_spec=grid_spec,
        out_shape=jax.ShapeDtypeStruct((n_rows, half), jnp.uint32),
        compiler_params=_cparams(("arbitrary",)),
        name="experts",
    )(blk_expert, n_used, slot, next_e, xs, wg, wu, wd)


def _combine_kernel(d1_ref, d2_ref, h_ref, w_ref, ys_ref, o_ref, a_ref, b_ref, sem, *, tm, n_steps):
    i = pl.program_id(0)
    slot = i % 2

    def gather(step, s):
        base = step * tm

        def row_copy(src, r, buf):
            return pltpu.make_async_copy(ys_ref.at[pl.ds(src, 1), :], buf.at[s, pl.ds(r, 1), :], sem.at[s])

        def issue(r, c):
            row_copy(d1_ref[base + r], r, a_ref).start()
            row_copy(d2_ref[base + r], r, b_ref).start()
            return c

        lax.fori_loop(0, tm, issue, 0, unroll=4)

    @pl.when(i == 0)
    def _():
        gather(0, 0)

    @pl.when(i + 1 < n_steps)
    def _():
        gather(i + 1, 1 - slot)

    for buf in (a_ref, b_ref):
        pltpu.make_async_copy(ys_ref.at[pl.ds(0, tm), :], buf.at[slot], sem.at[slot]).wait()

    half = a_ref.shape[2]
    rows = 8

    def slab(r, c):
        r0 = pl.multiple_of(r * rows, rows)
        a_lo, a_hi = _unpack_bf16_pair(a_ref[slot, pl.ds(r0, rows), :])
        b_lo, b_hi = _unpack_bf16_pair(b_ref[slot, pl.ds(r0, rows), :])
        w1 = _lanes(w_ref[pl.ds(r0, rows), :128], half)
        w2 = _lanes(w_ref[pl.ds(r0, rows), 128:], half)
        o_ref[pl.ds(r0, rows), :half] = h_ref[pl.ds(r0, rows), :half] + (w1 * a_lo + w2 * b_lo)
        o_ref[pl.ds(r0, rows), half:] = h_ref[pl.ds(r0, rows), half:] + (w1 * a_hi + w2 * b_hi)
        return c

    lax.fori_loop(0, tm // rows, slab, 0, unroll=2)


def _combine(d1, d2, h1, wcol, ys, tm):
    n, d = h1.shape
    half = d // 2
    grid_spec = pltpu.PrefetchScalarGridSpec(
        num_scalar_prefetch=2,
        grid=(n // tm,),
        in_specs=[
            pl.BlockSpec((tm, d), lambda i, *_: (i, 0)),
            pl.BlockSpec((tm, 256), lambda i, *_: (i, 0)),
            pl.BlockSpec(memory_space=pl.ANY),
        ],
        out_specs=pl.BlockSpec((tm, d), lambda i, *_: (i, 0)),
        scratch_shapes=[
            pltpu.VMEM((2, tm, half), jnp.uint32),
            pltpu.VMEM((2, tm, half), jnp.uint32),
            pltpu.SemaphoreType.DMA((2,)),
        ],
    )
    return pl.pallas_call(
        functools.partial(_combine_kernel, tm=tm, n_steps=n // tm),
        grid_spec=grid_spec,
        out_shape=jax.ShapeDtypeStruct((n, d), F32),
        compiler_params=_cparams(("arbitrary",)),
        name="combine",
    )(d1, d2, h1, wcol, ys)


def _rope_tab(pos):
    inv = 1.0 / (ROPE_THETA ** (jnp.arange(0, QK_ROPE, 2, dtype=F32) / QK_ROPE))
    ang = pos.astype(F32)[:, None] * inv[None, :]
    c = jnp.cos(ang)
    s = jnp.sin(ang)
    cos64 = jnp.concatenate([c, c], axis=-1)
    sin64 = jnp.concatenate([-s, s], axis=-1)
    tab_q = jnp.concatenate([cos64, sin64], axis=-1)
    tab_k = jnp.concatenate([cos64, cos64, sin64, sin64], axis=-1)
    return tab_q, tab_k


def _swap_halves(a):
    h = a.shape[-1] // 2
    return jnp.concatenate([a[..., h:], a[..., :h]], axis=-1)


def _tile(n, pref):
    t = pref
    while n % t:
        t //= 2
    return t


def kernel(x, meta_tokens, mix_norm_g, w_in, q_lat_norm_g, w_uq, kv_lat_norm_g, w_ukv, q_head_norm_g,
           k_head_norm_g, w_pool, pool_scale, w_out, ffn_norm_g, w_group, b_group, w_expert, b_expert,
           w_gate, w_up, w_down):
    batch, seq, d = x.shape
    depth = w_in.shape[0]
    pool_w = pool_scale.shape[-1]
    q_rank = q_lat_norm_g.shape[-1]
    n_heads = w_uq.shape[-1] // QK_HEAD
    n = batch * seq
    f = w_gate.shape[-1]

    tab_q, tab_k = _rope_tab(N_META + jnp.arange(seq))
    tab_qt = tab_q.T
    meta_rows = 128
    _, mtab_k = _rope_tab(jnp.arange(meta_rows))
    tm_exp = _tile(n, 256)
    n_blocks = (2 * n + N_EXPERTS * (tm_exp - 1) + tm_exp - 1) // tm_exp
    n_rows = n_blocks * tm_exp

    h = x.reshape(n, d)
    meta = meta_tokens.astype(F32)
    for l in range(depth):
        k_r = w_in[l][:, pool_w + q_rank + KV_LORA:]
        k_rs = _swap_halves(k_r)
        w_in_p = jnp.concatenate([w_in[l][:, :pool_w + q_rank + KV_LORA], k_r, k_r, k_rs, k_rs],
                                 axis=-1).astype(BF16)
        wq = w_uq[l].reshape(q_rank, n_heads, QK_HEAD)
        wuq_p = jnp.concatenate([wq, _swap_halves(wq[..., QK_NOPE:])], axis=-1)
        wqt = wuq_p.reshape(q_rank, n_heads * HEAD_PAD).T.astype(BF16)
        wkv = w_ukv[l].reshape(KV_LORA, n_heads, QK_NOPE + V_HEAD)
        wk = wkv[..., :QK_NOPE].reshape(KV_LORA, n_heads * QK_NOPE).astype(BF16)
        wvt = wkv[..., QK_NOPE:].reshape(KV_LORA, n_heads * V_HEAD).T.astype(BF16)
        gq = q_head_norm_g[l]
        gk = k_head_norm_g[l]
        gqc = jnp.broadcast_to(jnp.concatenate([gq, _swap_halves(gq[QK_NOPE:])])[:, None], (HEAD_PAD, 128))
        gk_r = gk[QK_NOPE:]
        gkh = jnp.concatenate([gk[:QK_NOPE], gk_r, gk_r, _swap_halves(gk_r), _swap_halves(gk_r)])[None, :]
        wr_t = jnp.zeros((ROUTER_ROWS, d), F32)
        wr_t = wr_t.at[:N_GROUPS].set(w_group[l].T).at[N_GROUPS:N_GROUPS + N_EXPERTS].set(w_expert[l].T)
        br = jnp.zeros((ROUTER_ROWS, 1), F32)
        br = br.at[:N_GROUPS, 0].set(b_group[l]).at[N_GROUPS:N_GROUPS + N_EXPERTS, 0].set(b_expert[l])

        u, lat, mu, mlat = _in_proj(h, meta, mix_norm_g[l][None, :], w_in_p, pool_w, _tile(n, 256))
        mlat_p = jnp.pad(mlat, ((0, meta_rows - N_META), (0, 0)))
        qt, k, vt, mk, mvt = _qkv(lat, tab_k, tab_qt, mlat_p, mtab_k, q_lat_norm_g[l][None, :],
                                  kv_lat_norm_g[l][None, :], gqc, gkh, wqt, wk, wvt,
                                  batch, seq, n_heads, q_rank, _tile(seq, 256))
        y_mla = _attention(qt, k, vt, mk, mvt, _tile(seq, ATTN_TQ)).reshape(n, n_heads * V_HEAD)
        h1 = _out_proj(u, mu, y_mla, h, w_pool[l].astype(BF16), pool_scale[l][None, :],
                       w_out[l].astype(BF16), seq, _tile(seq, 256), _tile(d, 2048))

        xp, ri, rf, cnt = _router(h1, ffn_norm_g[l][None, :], wr_t.astype(BF16), br, _tile(n, 512))
        counts = cnt[:, 0].astype(jnp.int32)
        padded = (counts + tm_exp - 1) // tm_exp * tm_exp
        pend = jnp.cumsum(padded)
        pstart = pend - padded
        n_used = (pend[-1] // tm_exp).astype(jnp.int32)
        blk_start = jnp.arange(n_blocks, dtype=jnp.int32) * tm_exp
        blk_expert = jnp.sum((pend[None, :] <= blk_start[:, None]).astype(jnp.int32), axis=1)
        blk_expert = jnp.minimum(blk_expert, N_EXPERTS - 1)
        npad = (pend[-1] - 2 * n).astype(jnp.int32)

        dest = _dest(ri, pstart.astype(F32)[:, None], _tile(n, 512))
        d1 = dest[0]
        d2 = dest[1]
        xs = _dispatch(d1, d2, counts, pstart.astype(jnp.int32), padded.astype(jnp.int32),
                       npad[None], n_used[None], xp, n_rows, _tile(n, 1024), tm_exp)
        ys = _experts(blk_expert, n_used[None], xs, w_gate[l], w_up[l], w_down[l], tm_exp)
        h = _combine(d1, d2, h1, rf, ys, _tile(n, 256))
        if l + 1 < depth:
            raise NotImplementedError("meta-token outputs are only produced for a single layer")
    return h.reshape(batch, seq, d)
```

```python
import functools

import jax
import jax.numpy as jnp
from jax import lax
from jax.experimental import pallas as pl
from jax.experimental.pallas import tpu as pltpu

F32 = jnp.float32
BF16 = jnp.bfloat16

N_META = 16
POOL_WINDOWS = (2, 4, 8, 16)
QK_NOPE = 128
QK_ROPE = 64
QK_HEAD = QK_NOPE + QK_ROPE
V_HEAD = 128
V_AUG = V_HEAD + 16
LOG2E = 1.4426950408889634
HEAD_PAD = 256
KV_LORA = 512
ROPE_THETA = 10000.0
N_GROUPS = 8
EXPERTS_PER_GROUP = 8
N_EXPERTS = N_GROUPS * EXPERTS_PER_GROUP
EPS = 1e-6
ROUTER_ROWS = 128
NEG_BIG = -1e30

VMEM_LIMIT = 56 * 1024 * 1024


def _cparams(sem, vmem=VMEM_LIMIT):
    return pltpu.CompilerParams(dimension_semantics=sem, vmem_limit_bytes=vmem)


def _rms(xf, g, n):
    ss = jnp.sum(xf * xf, axis=-1, keepdims=True)
    return xf * lax.rsqrt(ss * (1.0 / n) + EPS) * g


def _in_proj_kernel(x_ref, meta_ref, g_ref, w_ref, u_ref, lat_ref, mu_ref, mlat_ref, *, n_u, d):
    def proj(xf):
        xn = _rms(xf, g_ref[...], d).astype(BF16)
        return jnp.dot(xn, w_ref[...], preferred_element_type=F32)

    p = proj(x_ref[...])
    u_ref[...] = p[:, :n_u]
    lat_ref[...] = p[:, n_u:]

    @pl.when(pl.program_id(0) == 0)
    def _():
        pm = proj(meta_ref[...])
        mu_ref[...] = pm[:, :n_u]
        mlat_ref[...] = pm[:, n_u:]


def _in_proj(x2d, meta, g, w_p, n_u, tm):
    n, d = x2d.shape
    n_out = w_p.shape[1]
    n_lat = n_out - n_u
    return pl.pallas_call(
        functools.partial(_in_proj_kernel, n_u=n_u, d=d),
        grid=(n // tm,),
        in_specs=[
            pl.BlockSpec((tm, d), lambda i: (i, 0)),
            pl.BlockSpec((N_META, d), lambda i: (0, 0)),
            pl.BlockSpec((1, d), lambda i: (0, 0)),
            pl.BlockSpec((d, n_out), lambda i: (0, 0), pipeline_mode=pl.Buffered(1)),
        ],
        out_specs=[
            pl.BlockSpec((tm, n_u), lambda i: (i, 0)),
            pl.BlockSpec((tm, n_lat), lambda i: (i, 0)),
            pl.BlockSpec((N_META, n_u), lambda i: (0, 0)),
            pl.BlockSpec((N_META, n_lat), lambda i: (0, 0)),
        ],
        out_shape=[
            jax.ShapeDtypeStruct((n, n_u), F32),
            jax.ShapeDtypeStruct((n, n_lat), F32),
            jax.ShapeDtypeStruct((N_META, n_u), F32),
            jax.ShapeDtypeStruct((N_META, n_lat), F32),
        ],
        compiler_params=_cparams(("arbitrary",)),
        name="in_proj",
    )(x2d, meta, g, w_p)


def _lanes(a, m):
    return jnp.concatenate([a] * (m // 128), axis=1) if m > 128 else a


def _ones_rows(r, m):
    return jnp.where(lax.broadcasted_iota(jnp.int32, (r, m), 0) == 0, 1.0, 0.0).astype(BF16)


def _kv_rows(lat, tab, gkvl_ref, gkh_ref, wk_ref, wvt_ref, store_k, store_vt, *, n_heads, q_rank):
    kvl = lat[:, q_rank:q_rank + KV_LORA]
    kd = lat[:, q_rank + KV_LORA:q_rank + KV_LORA + 128]
    kr = lat[:, q_rank + KV_LORA + 128:q_rank + KV_LORA + 256]
    kvn = _rms(kvl, gkvl_ref[...], KV_LORA)
    kvn_b = kvn.astype(BF16)
    kvn_t = kvn.T.astype(BF16)
    gk0 = gkh_ref[:, :128]
    k_rope = kd * gkh_ref[:, 128:256] * tab[:, :128] + kr * gkh_ref[:, 256:384] * tab[:, 128:256]
    ss_kr = 0.5 * jnp.sum(kd * kd, axis=-1, keepdims=True)
    inv_n = 1.0 / QK_HEAD
    for p in range(n_heads // 2):
        kk = jnp.dot(kvn_b, wk_ref[:, p * 256:(p + 1) * 256], preferred_element_type=F32)
        vv = jnp.dot(wvt_ref[p * 256:(p + 1) * 256, :], kvn_t, preferred_element_type=F32)
        for t in range(2):
            kn = kk[:, t * 128:(t + 1) * 128]
            ssk = jnp.sum(kn * kn, axis=-1, keepdims=True) + ss_kr
            rk = lax.rsqrt(ssk * inv_n + EPS)
            store_k(2 * p + t, jnp.concatenate([kn * rk * gk0, k_rope * rk], axis=-1).astype(BF16))
            store_vt(2 * p + t, vv[t * 128:(t + 1) * 128].astype(BF16))


def _qkv_kernel(lat_ref, tab_ref, tqt_ref, mlat_ref, mtab_ref, gql_ref, gkvl_ref, gqc_ref, gkh_ref,
                wqt_ref, wk_ref, wvt_ref, qt_ref, k_ref, vt_ref, mk_ref, mvt_ref, *, n_heads, q_rank):
    lat = lat_ref[...]
    m = lat.shape[0]

    def sk(h, val):
        k_ref[h] = val

    def svt(h, val):
        vt_ref[h, :V_HEAD, :] = val
        vt_ref[h, V_HEAD:, :] = _ones_rows(V_AUG - V_HEAD, val.shape[1])

    _kv_rows(lat, tab_ref[...], gkvl_ref, gkh_ref, wk_ref, wvt_ref, sk, svt, n_heads=n_heads, q_rank=q_rank)

    scale = QK_HEAD ** -0.5 * LOG2E
    qn_t = _rms(lat[:, :q_rank], gql_ref[...], q_rank).T.astype(BF16)
    gq0 = _lanes(gqc_ref[:128, :], m) * scale
    gq1 = _lanes(gqc_ref[128:, :], m) * tqt_ref[...] * scale
    inv_n = 1.0 / QK_HEAD
    for h in range(n_heads):
        xh = jnp.dot(wqt_ref[h * HEAD_PAD:(h + 1) * HEAD_PAD, :], qn_t, preferred_element_type=F32)
        x0 = xh[:128]
        x1 = xh[128:]
        ss = jnp.sum(x0 * x0, axis=0, keepdims=True) + 0.5 * jnp.sum(x1 * x1, axis=0, keepdims=True)
        rinv = lax.rsqrt(ss * inv_n + EPS)
        qt_ref[h] = jnp.concatenate([x0 * rinv * gq0, x1 * rinv * gq1], axis=0).astype(BF16)

    @pl.when((pl.program_id(0) == 0) & (pl.program_id(1) == 0))
    def _():
        def smk(h, val):
            mk_ref[h] = val

        def smvt(h, val):
            mvt_ref[h, :V_HEAD, :] = val
            mvt_ref[h, V_HEAD:, :] = _ones_rows(V_AUG - V_HEAD, val.shape[1])

        _kv_rows(mlat_ref[...], mtab_ref[...], gkvl_ref, gkh_ref, wk_ref, wvt_ref, smk, smvt,
                 n_heads=n_heads, q_rank=q_rank)


def _qkv(lat, tab, tqt, mlat, mtab, gql, gkvl, gqc, gkh, wqt, wk, wvt, batch, seq, n_heads, q_rank, tm):
    n_lat = lat.shape[1]
    tpb = seq // tm
    mp = mlat.shape[0]
    const = lambda b, i: (0, 0)
    return pl.pallas_call(
        functools.partial(_qkv_kernel, n_heads=n_heads, q_rank=q_rank),
        grid=(batch, tpb),
        in_specs=[
            pl.BlockSpec((tm, n_lat), lambda b, i: (b * tpb + i, 0)),
            pl.BlockSpec((tm, 256), lambda b, i: (i, 0)),
            pl.BlockSpec((128, tm), lambda b, i: (0, i)),
            pl.BlockSpec((mp, n_lat), const),
            pl.BlockSpec((mp, 256), const),
            pl.BlockSpec((1, q_rank), const),
            pl.BlockSpec((1, KV_LORA), const),
            pl.BlockSpec((256, 128), const),
            pl.BlockSpec((1, 384), const),
            pl.BlockSpec(wqt.shape, const, pipeline_mode=pl.Buffered(1)),
            pl.BlockSpec(wk.shape, const, pipeline_mode=pl.Buffered(1)),
            pl.BlockSpec(wvt.shape, const, pipeline_mode=pl.Buffered(1)),
        ],
        out_specs=[
            pl.BlockSpec((None, n_heads, HEAD_PAD, tm), lambda b, i: (b, 0, 0, i)),
            pl.BlockSpec((None, n_heads, tm, HEAD_PAD), lambda b, i: (b, 0, i, 0)),
            pl.BlockSpec((None, n_heads, None, V_AUG, tm), lambda b, i: (b, 0, i, 0, 0)),
            pl.BlockSpec((n_heads, mp, HEAD_PAD), lambda b, i: (0, 0, 0)),
            pl.BlockSpec((n_heads, V_AUG, mp), lambda b, i: (0, 0, 0)),
        ],
        out_shape=[
            jax.ShapeDtypeStruct((batch, n_heads, HEAD_PAD, seq), BF16),
            jax.ShapeDtypeStruct((batch, n_heads, seq, HEAD_PAD), BF16),
            jax.ShapeDtypeStruct((batch, n_heads, tpb, V_AUG, tm), BF16),
            jax.ShapeDtypeStruct((n_heads, mp, HEAD_PAD), BF16),
            jax.ShapeDtypeStruct((n_heads, V_AUG, mp), BF16),
        ],
        compiler_params=_cparams(("arbitrary", "arbitrary")),
        name="qkv",
    )(lat, tab, tqt, mlat, mtab, gql, gkvl, gqc, gkh, wqt, wk, wvt)


ATTN_HEADS = 2
ATTN_STRIP = 256
ATTN_TQ = 2048
ATTN_TK = 1024
ATTN_SM_LAG = 1
ATTN_PV_LAG = 3


def _attn_kernel(qt_ref, k_ref, vt_ref, mk_ref, mvt_ref, o_ref, acc_ref, *, tq, tv):
    qi = pl.program_id(2)
    nh = qt_ref.shape[0]
    sw = ATTN_STRIP
    chains = [(hh, x) for x in range(tq // sw) for hh in range(nh)]
    qts = [qt_ref[hh, :, x * sw:(x + 1) * sw] for hh, x in chains]

    carry = []
    for ci, (hh, x) in enumerate(chains):
        s0 = jnp.dot(mk_ref[hh], qts[ci], preferred_element_type=F32)
        s0 = jnp.where(lax.broadcasted_iota(jnp.int32, s0.shape, 0) < N_META, s0, NEG_BIG)
        m0 = jnp.max(s0, axis=0, keepdims=True)
        carry.append(m0)
        acc_ref[hh, :, x * sw:(x + 1) * sw] = jnp.dot(mvt_ref[hh], jnp.exp2(s0 - m0).astype(BF16),
                                                     preferred_element_type=F32)

    tk = ATTN_TK
    kpq = tq // tk

    def n_keys(x, diag):
        if diag is None:
            return tk
        return max(0, min(tk, (x + 1) * sw - diag * tk))

    def scores(ci, j, diag):
        hh, x = chains[ci]
        start = pl.multiple_of(j * tk, tk)
        s = jnp.dot(k_ref[hh, pl.ds(start, n_keys(x, diag)), :], qts[ci],
                    preferred_element_type=F32)
        if diag is not None and (diag + 1) * tk - 1 > x * sw:
            key = lax.broadcasted_iota(jnp.int32, s.shape, 0) + diag * tk
            qry = lax.broadcasted_iota(jnp.int32, s.shape, 1) + x * sw
            s = jnp.where(key <= qry, s, NEG_BIG)
        return s

    def softmax(s, m_prev):
        m_new = jnp.maximum(m_prev, jnp.max(s, axis=0, keepdims=True))
        return m_new, jnp.exp2(m_prev - m_new), jnp.exp2((s - m_new).astype(BF16))

    def values(ci, j, alpha, pb):
        hh, x = chains[ci]
        pv = jnp.dot(vt_ref[hh, j * (tk // tv)], pb[:tv], preferred_element_type=F32)
        for c in range(1, pb.shape[0] // tv):
            pv += jnp.dot(vt_ref[hh, j * (tk // tv) + c], pb[c * tv:(c + 1) * tv],
                          preferred_element_type=F32)
        acc_ref[hh, :, x * sw:(x + 1) * sw] = alpha * acc_ref[hh, :, x * sw:(x + 1) * sw] + pv

    def step(j, c, diag=None):
        live = [ci for ci, (hh, x) in enumerate(chains) if n_keys(x, diag) > 0]
        n = len(live)
        out = list(c)
        s_vals = {}
        sm_vals = {}
        lag = ATTN_PV_LAG
        for t in range(n + lag):
            if t < n:
                s_vals[t] = scores(live[t], j, diag)
            if 0 <= t - ATTN_SM_LAG < n:
                ts = t - ATTN_SM_LAG
                ci = live[ts]
                out[ci], alpha, pb = softmax(s_vals.pop(ts), c[ci])
                sm_vals[ts] = (alpha, pb)
            if 0 <= t - lag < n:
                values(live[t - lag], j, *sm_vals.pop(t - lag))
        return tuple(out)

    c = lax.fori_loop(0, qi * kpq, lambda j, c: step(j, c), tuple(carry))
    for diag in range(kpq):
        c = step(qi * kpq + diag, c, diag)
    for hh, x in chains:
        acc = acc_ref[hh, :, x * sw:(x + 1) * sw]
        o_ref[x * sw:(x + 1) * sw, hh * V_HEAD:(hh + 1) * V_HEAD] = (
            acc[:V_HEAD] / acc[V_HEAD:V_HEAD + 1]).T.astype(o_ref.dtype)


def _attention(qt, k, vt, mk, mvt, tq):
    batch, n_heads, seq, _ = k.shape
    n_chunks, tv = vt.shape[2], vt.shape[4]
    mp = mk.shape[1]
    nh = ATTN_HEADS
    return pl.pallas_call(
        functools.partial(_attn_kernel, tq=tq, tv=tv),
        grid=(batch, n_heads // nh, seq // tq),
        in_specs=[
            pl.BlockSpec((None, nh, HEAD_PAD, tq), lambda b, h, i: (b, h, 0, i)),
            pl.BlockSpec((None, nh, seq, HEAD_PAD), lambda b, h, i: (b, h, 0, 0)),
            pl.BlockSpec((None, nh, n_chunks, V_AUG, tv), lambda b, h, i: (b, h, 0, 0, 0)),
            pl.BlockSpec((nh, mp, HEAD_PAD), lambda b, h, i: (h, 0, 0)),
            pl.BlockSpec((nh, V_AUG, mp), lambda b, h, i: (h, 0, 0)),
        ],
        out_specs=pl.BlockSpec((None, tq, nh * V_HEAD), lambda b, h, i: (b, i, h)),
        out_shape=jax.ShapeDtypeStruct((batch, seq, n_heads * V_HEAD), BF16),
        scratch_shapes=[pltpu.VMEM((nh, V_AUG, tq), F32)],
        compiler_params=_cparams(("arbitrary", "arbitrary", "arbitrary")),
        name="attention",
    )(qt, k, vt, mk, mvt)


def _out_proj_kernel(u_ref, halo_ref, mu_ref, y_ref, x_ref, wp_ref, ps_ref, wo_ref, o_ref,
                     *, tiles_per_batch, group):
    i = pl.program_id(1)
    first = (i % tiles_per_batch) == 0
    halo = jnp.where(first, mu_ref[...], halo_ref[...])
    ucat = jnp.concatenate([halo, u_ref[...]], axis=0)
    ys = []
    for gi, w in enumerate(POOL_WINDOWS):
        ug = ucat[:, gi * group:(gi + 1) * group]
        s = ug
        shift = 1
        while shift < w:
            s = s + pltpu.roll(s, shift, 0)
            shift *= 2
        diff = (s * (1.0 / w) - ug)[N_META:]
        yg = jnp.dot(diff.astype(BF16), wp_ref[gi], preferred_element_type=F32)
        ys.append((yg * ps_ref[:, gi * group:(gi + 1) * group]).astype(BF16))
    ycat = jnp.concatenate(ys + [y_ref[...]], axis=-1)
    o_ref[...] = x_ref[...] + jnp.dot(ycat, wo_ref[...], preferred_element_type=F32)


def _out_proj(u, mu, y_mla, x2d, w_pool, pool_scale, w_out, seq, tm, tn):
    n, d = x2d.shape
    pool_w = u.shape[1]
    mla_w = y_mla.shape[1]
    group = pool_w // len(POOL_WINDOWS)
    hb = tm // N_META
    return pl.pallas_call(
        functools.partial(_out_proj_kernel, tiles_per_batch=seq // tm, group=group),
        grid=(d // tn, n // tm),
        in_specs=[
            pl.BlockSpec((tm, pool_w), lambda j, i: (i, 0)),
            pl.BlockSpec((N_META, pool_w), lambda j, i: (jnp.maximum(i * hb - 1, 0), 0)),
            pl.BlockSpec((N_META, pool_w), lambda j, i: (0, 0)),
            pl.BlockSpec((tm, mla_w), lambda j, i: (i, 0)),
            pl.BlockSpec((tm, tn), lambda j, i: (i, j)),
            pl.BlockSpec(w_pool.shape, lambda j, i: (0, 0, 0)),
            pl.BlockSpec((1, pool_w), lambda j, i: (0, 0)),
            pl.BlockSpec((d, tn), lambda j, i: (0, j)),
        ],
        out_specs=pl.BlockSpec((tm, tn), lambda j, i: (i, j)),
        out_shape=jax.ShapeDtypeStruct((n, d), F32),
        compiler_params=_cparams(("arbitrary", "arbitrary")),
        name="out_proj",
    )(u, u, mu, y_mla, x2d, w_pool, pool_scale, w_out)


def _pack_bf16_pair(lo, hi):
    lo_b = lax.bitcast_convert_type(lo.astype(BF16).astype(F32), jnp.uint32)
    hi_b = lax.bitcast_convert_type(hi.astype(BF16).astype(F32), jnp.uint32)
    return lax.shift_right_logical(lo_b, jnp.uint32(16)) | hi_b


def _unpack_bf16_pair(w):
    lo = lax.bitcast_convert_type(lax.shift_left(w, jnp.uint32(16)), F32)
    hi = lax.bitcast_convert_type(w & jnp.uint32(0xFFFF0000), F32)
    return lo, hi


def _first_argmax(vals, sub, n):
    mx = jnp.max(vals, axis=0, keepdims=True)
    idx = jnp.min(jnp.where(vals == mx, sub, n), axis=0, keepdims=True)
    return mx, idx


def _router_kernel(h_ref, g_ref, wr_ref, br_ref, xp_ref, ri_ref, rf_ref, cnt_ref, *, d):
    xn = _rms(h_ref[...], g_ref[...], d)
    half = d // 2
    xp_ref[...] = _pack_bf16_pair(xn[:, :half], xn[:, half:])
    logits = lax.dot_general(wr_ref[...], xn.astype(BF16), (((1,), (1,)), ((), ())),
                             preferred_element_type=F32) + br_ref[...]
    tm = logits.shape[1]
    sub = lax.broadcasted_iota(jnp.int32, (N_GROUPS, tm), 0)
    lg = logits[:N_GROUPS]
    gmax, g_idx = _first_argmax(lg, sub, N_GROUPS)
    g_p = 1.0 / jnp.sum(jnp.exp(lg - gmax), axis=0, keepdims=True)
    in_group = jnp.zeros((EXPERTS_PER_GROUP, tm), F32)
    for g in range(N_GROUPS):
        lo = N_GROUPS + g * EXPERTS_PER_GROUP
        in_group = jnp.where(g_idx == g, logits[lo:lo + EXPERTS_PER_GROUP], in_group)
    m1, i1 = _first_argmax(in_group, sub, EXPERTS_PER_GROUP)
    rest = jnp.where(sub == i1, -jnp.inf, in_group)
    m2, i2 = _first_argmax(rest, sub, EXPERTS_PER_GROUP)
    e = jnp.exp(m2 - m1)
    w1 = g_p / (1.0 + e)
    w2 = g_p * e / (1.0 + e)
    e1 = g_idx * EXPERTS_PER_GROUP + i1
    e2 = g_idx * EXPERTS_PER_GROUP + i2
    ri_ref[...] = jnp.where(sub == 0, e1, jnp.where(sub == 1, e2, 0))
    rf_ref[:, :128] = jnp.broadcast_to(w1, (128, tm)).T
    rf_ref[:, 128:] = jnp.broadcast_to(w2, (128, tm)).T
    sub_e = lax.broadcasted_iota(jnp.int32, (N_EXPERTS, tm), 0)
    hit = jnp.where((sub_e == e1) | (sub_e == e2), 1.0, 0.0)
    cnt = jnp.sum(hit, axis=1, keepdims=True)

    @pl.when(pl.program_id(0) == 0)
    def _():
        cnt_ref[...] = jnp.zeros_like(cnt_ref)

    cnt_ref[...] += jnp.broadcast_to(cnt, cnt_ref.shape)


def _router(h1, g, wr_t, br, tm):
    n, d = h1.shape
    return pl.pallas_call(
        functools.partial(_router_kernel, d=d),
        grid=(n // tm,),
        in_specs=[
            pl.BlockSpec((tm, d), lambda i: (i, 0)),
            pl.BlockSpec((1, d), lambda i: (0, 0)),
            pl.BlockSpec((ROUTER_ROWS, d), lambda i: (0, 0)),
            pl.BlockSpec((ROUTER_ROWS, 1), lambda i: (0, 0)),
        ],
        out_specs=[
            pl.BlockSpec((tm, d // 2), lambda i: (i, 0)),
            pl.BlockSpec((8, tm), lambda i: (0, i)),
            pl.BlockSpec((tm, 256), lambda i: (i, 0)),
            pl.BlockSpec((N_EXPERTS, 128), lambda i: (0, 0)),
        ],
        out_shape=[
            jax.ShapeDtypeStruct((n, d // 2), jnp.uint32),
            jax.ShapeDtypeStruct((8, n), jnp.int32),
            jax.ShapeDtypeStruct((n, 256), F32),
            jax.ShapeDtypeStruct((N_EXPERTS, 128), F32),
        ],
        compiler_params=_cparams(("arbitrary",)),
        name="router",
    )(h1, g, wr_t, br)


def _dest_kernel(ri_ref, pstart_ref, tri_ref, d_ref, carry_ref):
    @pl.when(pl.program_id(0) == 0)
    def _():
        carry_ref[...] = jnp.zeros_like(carry_ref)

    tn = ri_ref.shape[1]
    e1 = ri_ref[0:1, :]
    e2 = ri_ref[1:2, :]
    sub = lax.broadcasted_iota(jnp.int32, (N_EXPERTS, tn), 0)
    oh1 = sub == e1
    oh2 = sub == e2
    oh = jnp.where(oh1 | oh2, 1.0, 0.0)
    c = jnp.dot(oh.astype(BF16), tri_ref[...], preferred_element_type=F32)
    val = c + (pstart_ref[...] + carry_ref[:, :1] - 1.0)
    d1 = jnp.sum(jnp.where(oh1, val, 0.0), axis=0, keepdims=True)
    d2 = jnp.sum(jnp.where(oh2, val, 0.0), axis=0, keepdims=True)
    sub8 = lax.broadcasted_iota(jnp.int32, (8, tn), 0)
    d_ref[...] = jnp.where(sub8 == 0, d1, jnp.where(sub8 == 1, d2, 0.0)).astype(jnp.int32)
    carry_ref[...] += jnp.broadcast_to(jnp.sum(oh, axis=1, keepdims=True), carry_ref.shape)


def _dest(ri, pstart_col, tn):
    n = ri.shape[1]
    tri = (jnp.arange(tn)[:, None] <= jnp.arange(tn)[None, :]).astype(BF16)
    return pl.pallas_call(
        _dest_kernel,
        grid=(n // tn,),
        in_specs=[
            pl.BlockSpec((8, tn), lambda i: (0, i)),
            pl.BlockSpec((N_EXPERTS, 1), lambda i: (0, 0)),
            pl.BlockSpec((tn, tn), lambda i: (0, 0)),
        ],
        out_specs=pl.BlockSpec((8, tn), lambda i: (0, i)),
        out_shape=jax.ShapeDtypeStruct((8, n), jnp.int32),
        scratch_shapes=[pltpu.VMEM((N_EXPERTS, 128), F32)],
        compiler_params=_cparams(("arbitrary",)),
        name="dest",
    )(ri, pstart_col, tri)


def _dispatch_kernel(d1_ref, d2_ref, cnt_ref, pstart_ref, padded_ref, npad_ref, nused_ref,
                     xp_ref, xs_ref, zero_ref, sem, zsem, bsem, *, tm, tb, n_blocks):
    i = pl.program_id(0)
    base = i * tm

    def row_copy(r, dst):
        return pltpu.make_async_copy(xp_ref.at[pl.ds(r, 1), :], xs_ref.at[pl.ds(dst, 1), :], sem)

    def zero_row_copy(dst):
        return pltpu.make_async_copy(zero_ref.at[pl.ds(0, 1), :], xs_ref.at[pl.ds(dst, 1), :], zsem)

    def zero_block_copy(blk):
        start = pl.multiple_of(blk * tb, tb)
        return pltpu.make_async_copy(zero_ref, xs_ref.at[pl.ds(start, tb), :], bsem)

    def issue(r, c):
        row_copy(r, d1_ref[base + r]).start()
        row_copy(r, d2_ref[base + r]).start()
        return c

    lax.fori_loop(0, tm, issue, 0, unroll=4)

    @pl.when(i == 0)
    def _():
        zero_ref[...] = jnp.zeros_like(zero_ref)

        def per_expert(e, c):
            lo = pstart_ref[e] + cnt_ref[e]
            hi = pstart_ref[e] + padded_ref[e]

            def per_row(p, c2):
                zero_row_copy(p).start()
                return c2

            lax.fori_loop(lo, hi, per_row, 0)
            return c

        lax.fori_loop(0, N_EXPERTS, per_expert, 0)

        def per_block(b, c):
            zero_block_copy(b).start()
            return c

        lax.fori_loop(nused_ref[0], n_blocks, per_block, 0)

        def zwait(p, c):
            zero_row_copy(0).wait()
            return c

        lax.fori_loop(0, npad_ref[0], zwait, 0)

        def bwait(b, c):
            zero_block_copy(0).wait()
            return c

        lax.fori_loop(nused_ref[0], n_blocks, bwait, 0)

    for _ in range(2):
        pltpu.make_async_copy(xp_ref, xs_ref.at[pl.ds(0, tm), :], sem).wait()


def _dispatch(d1, d2, cnt, pstart, padded, npad, n_used, xp, n_rows, tm, tb):
    n, half = xp.shape
    grid_spec = pltpu.PrefetchScalarGridSpec(
        num_scalar_prefetch=7,
        grid=(n // tm,),
        in_specs=[pl.BlockSpec((tm, half), lambda i, *_: (i, 0))],
        out_specs=pl.BlockSpec(memory_space=pl.ANY),
        scratch_shapes=[
            pltpu.VMEM((tb, half), jnp.uint32),
            pltpu.SemaphoreType.DMA,
            pltpu.SemaphoreType.DMA,
            pltpu.SemaphoreType.DMA,
        ],
    )
    return pl.pallas_call(
        functools.partial(_dispatch_kernel, tm=tm, tb=tb, n_blocks=n_rows // tb),
        grid_spec=grid_spec,
        out_shape=jax.ShapeDtypeStruct((n_rows, half), jnp.uint32),
        compiler_params=pltpu.CompilerParams(
            dimension_semantics=("arbitrary",), vmem_limit_bytes=VMEM_LIMIT, has_side_effects=True),
        name="dispatch",
    )(d1, d2, cnt, pstart, padded, npad, n_used, xp)


def _experts_kernel(be_ref, nused_ref, slot_ref, nexte_ref, xs_ref, wg_hbm, wu_hbm, wd_hbm, ys_ref,
                    wg_ref, wu_ref, wd_ref, w_sem):
    i = pl.program_id(0)
    active = i < nused_ref[0]
    fresh = (i == 0) | (be_ref[i] != be_ref[jnp.maximum(i - 1, 0)])
    slot = slot_ref[i]

    def w_copies(e, s):
        return (pltpu.make_async_copy(wg_hbm.at[e], wg_ref.at[s], w_sem.at[s]),
                pltpu.make_async_copy(wu_hbm.at[e], wu_ref.at[s], w_sem.at[s]),
                pltpu.make_async_copy(wd_hbm.at[e], wd_ref.at[s], w_sem.at[s]))

    @pl.when(active & (i == 0))
    def _():
        for cp in w_copies(be_ref[0], 0):
            cp.start()

    @pl.when(active & fresh)
    def _():
        for cp in w_copies(be_ref[i], slot):
            cp.wait()

        @pl.when(nexte_ref[i] >= 0)
        def _():
            for cp in w_copies(nexte_ref[i], 1 - slot):
                cp.start()

    @pl.when(active)
    def _():
        lo, hi = _unpack_bf16_pair(xs_ref[...])
        half = lo.shape[1]
        g = (jnp.dot(lo, wg_ref[slot, :half], preferred_element_type=F32)
             + jnp.dot(hi, wg_ref[slot, half:], preferred_element_type=F32))
        u = (jnp.dot(lo, wu_ref[slot, :half], preferred_element_type=F32)
             + jnp.dot(hi, wu_ref[slot, half:], preferred_element_type=F32))
        hb = g / (1.0 + jnp.exp(-g)) * u
        y = jnp.dot(hb, wd_ref[slot], preferred_element_type=F32)
        ys_ref[...] = _pack_bf16_pair(y[:, :half], y[:, half:])

    @pl.when(i >= nused_ref[0])
    def _():
        ys_ref[...] = jnp.zeros_like(ys_ref)


def _experts(blk_expert, n_used, xs, wg, wu, wd, tm):
    n_rows, half = xs.shape
    _, d, f = wg.shape
    n_blocks = n_rows // tm

    idx = jnp.arange(n_blocks, dtype=jnp.int32)
    fresh = jnp.concatenate([jnp.ones((1,), bool), blk_expert[1:] != blk_expert[:-1]]) & (idx < n_used[0])
    slot = ((jnp.cumsum(fresh.astype(jnp.int32)) - 1) % 2).astype(jnp.int32)
    later = jnp.where(fresh, idx, n_blocks)
    nxt = lax.cummin(jnp.concatenate([later[1:], jnp.full((1,), n_blocks, jnp.int32)]), reverse=True)
    next_e = jnp.where(nxt < n_blocks, blk_expert[jnp.minimum(nxt, n_blocks - 1)], -1).astype(jnp.int32)

    def row_map(i, *_):
        return (i, 0)

    grid_spec = pltpu.PrefetchScalarGridSpec(
        num_scalar_prefetch=4,
        grid=(n_blocks,),
        in_specs=[
            pl.BlockSpec((tm, half), row_map),
            pl.BlockSpec(memory_space=pl.ANY),
            pl.BlockSpec(memory_space=pl.ANY),
            pl.BlockSpec(memory_space=pl.ANY),
        ],
        out_specs=pl.BlockSpec((tm, half), row_map),
        scratch_shapes=[
            pltpu.VMEM((2, d, f), F32),
            pltpu.VMEM((2, d, f), F32),
            pltpu.VMEM((2, f, d), F32),
            pltpu.SemaphoreType.DMA((2,)),
        ],
    )
    return pl.pallas_call(
        _experts_kernel,
        grid_spec=grid_spec,
        out_shape=jax.ShapeDtypeStruct((n_rows, half), jnp.uint32),
        compiler_params=_cparams(("arbitrary",)),
        name="experts",
    )(blk_expert, n_used, slot, next_e, xs, wg, wu, wd)


def _combine_kernel(d1_ref, d2_ref, h_ref, w_ref, ys_ref, o_ref, a_ref, b_ref, sem, *, tm, n_steps):
    i = pl.program_id(0)
    slot = i % 2

    def gather(step, s):
        base = step * tm

        def row_copy(src, r, buf):
            return pltpu.make_async_copy(ys_ref.at[pl.ds(src, 1), :], buf.at[s, pl.ds(r, 1), :], sem.at[s])

        def issue(r, c):
            row_copy(d1_ref[base + r], r, a_ref).start()
            row_copy(d2_ref[base + r], r, b_ref).start()
            return c

        lax.fori_loop(0, tm, issue, 0, unroll=4)

    @pl.when(i == 0)
    def _():
        gather(0, 0)

    @pl.when(i + 1 < n_steps)
    def _():
        gather(i + 1, 1 - slot)

    for buf in (a_ref, b_ref):
        pltpu.make_async_copy(ys_ref.at[pl.ds(0, tm), :], buf.at[slot], sem.at[slot]).wait()

    half = a_ref.shape[2]
    rows = 8

    def slab(r, c):
        r0 = pl.multiple_of(r * rows, rows)
        a_lo, a_hi = _unpack_bf16_pair(a_ref[slot, pl.ds(r0, rows), :])
        b_lo, b_hi = _unpack_bf16_pair(b_ref[slot, pl.ds(r0, rows), :])
        w1 = _lanes(w_ref[pl.ds(r0, rows), :128], half)
        w2 = _lanes(w_ref[pl.ds(r0, rows), 128:], half)
        o_ref[pl.ds(r0, rows), :half] = h_ref[pl.ds(r0, rows), :half] + (w1 * a_lo + w2 * b_lo)
        o_ref[pl.ds(r0, rows), half:] = h_ref[pl.ds(r0, rows), half:] + (w1 * a_hi + w2 * b_hi)
        return c

    lax.fori_loop(0, tm // rows, slab, 0, unroll=2)


def _combine(d1, d2, h1, wcol, ys, tm):
    n, d = h1.shape
    half = d // 2
    grid_spec = pltpu.PrefetchScalarGridSpec(
        num_scalar_prefetch=2,
        grid=(n // tm,),
        in_specs=[
            pl.BlockSpec((tm, d), lambda i, *_: (i, 0)),
            pl.BlockSpec((tm, 256), lambda i, *_: (i, 0)),
            pl.BlockSpec(memory_space=pl.ANY),
        ],
        out_specs=pl.BlockSpec((tm, d), lambda i, *_: (i, 0)),
        scratch_shapes=[
            pltpu.VMEM((2, tm, half), jnp.uint32),
            pltpu.VMEM((2, tm, half), jnp.uint32),
            pltpu.SemaphoreType.DMA((2,)),
        ],
    )
    return pl.pallas_call(
        functools.partial(_combine_kernel, tm=tm, n_steps=n // tm),
        grid_spec=grid_spec,
        out_shape=jax.ShapeDtypeStruct((n, d), F32),
        compiler_params=_cparams(("arbitrary",)),
        name="combine",
    )(d1, d2, h1, wcol, ys)


def _rope_tab(pos):
    inv = 1.0 / (ROPE_THETA ** (jnp.arange(0, QK_ROPE, 2, dtype=F32) / QK_ROPE))
    ang = pos.astype(F32)[:, None] * inv[None, :]
    c = jnp.cos(ang)
    s = jnp.sin(ang)
    cos64 = jnp.concatenate([c, c], axis=-1)
    sin64 = jnp.concatenate([-s, s], axis=-1)
    tab_q = jnp.concatenate([cos64, sin64], axis=-1)
    tab_k = jnp.concatenate([cos64, cos64, sin64, sin64], axis=-1)
    return tab_q, tab_k


def _swap_halves(a):
    h = a.shape[-1] // 2
    return jnp.concatenate([a[..., h:], a[..., :h]], axis=-1)


def _tile(n, pref):
    t = pref
    while n % t:
        t //= 2
    return t


def kernel(x, meta_tokens, mix_norm_g, w_in, q_lat_norm_g, w_uq, kv_lat_norm_g, w_ukv, q_head_norm_g,
           k_head_norm_g, w_pool, pool_scale, w_out, ffn_norm_g, w_group, b_group, w_expert, b_expert,
           w_gate, w_up, w_down):
    batch, seq, d = x.shape
    depth = w_in.shape[0]
    pool_w = pool_scale.shape[-1]
    q_rank = q_lat_norm_g.shape[-1]
    n_heads = w_uq.shape[-1] // QK_HEAD
    n = batch * seq
    f = w_gate.shape[-1]

    tab_q, tab_k = _rope_tab(N_META + jnp.arange(seq))
    tab_qt = tab_q.T
    meta_rows = 128
    _, mtab_k = _rope_tab(jnp.arange(meta_rows))
    tm_exp = _tile(n, 128)
    n_blocks = (2 * n + N_EXPERTS * (tm_exp - 1) + tm_exp - 1) // tm_exp
    n_rows = n_blocks * tm_exp

    h = x.reshape(n, d)
    meta = meta_tokens.astype(F32)
    for l in range(depth):
        k_r = w_in[l][:, pool_w + q_rank + KV_LORA:]
        k_rs = _swap_halves(k_r)
        w_in_p = jnp.concatenate([w_in[l][:, :pool_w + q_rank + KV_LORA], k_r, k_r, k_rs, k_rs],
                                 axis=-1).astype(BF16)
        wq = w_uq[l].reshape(q_rank, n_heads, QK_HEAD)
        wuq_p = jnp.concatenate([wq, _swap_halves(wq[..., QK_NOPE:])], axis=-1)
        wqt = wuq_p.reshape(q_rank, n_heads * HEAD_PAD).T.astype(BF16)
        wkv = w_ukv[l].reshape(KV_LORA, n_heads, QK_NOPE + V_HEAD)
        wk = wkv[..., :QK_NOPE].reshape(KV_LORA, n_heads * QK_NOPE).astype(BF16)
        wvt = wkv[..., QK_NOPE:].reshape(KV_LORA, n_heads * V_HEAD).T.astype(BF16)
        gq = q_head_norm_g[l]
        gk = k_head_norm_g[l]
        gqc = jnp.broadcast_to(jnp.concatenate([gq, _swap_halves(gq[QK_NOPE:])])[:, None], (HEAD_PAD, 128))
        gk_r = gk[QK_NOPE:]
        gkh = jnp.concatenate([gk[:QK_NOPE], gk_r, gk_r, _swap_halves(gk_r), _swap_halves(gk_r)])[None, :]
        wr_t = jnp.zeros((ROUTER_ROWS, d), F32)
        wr_t = wr_t.at[:N_GROUPS].set(w_group[l].T).at[N_GROUPS:N_GROUPS + N_EXPERTS].set(w_expert[l].T)
        br = jnp.zeros((ROUTER_ROWS, 1), F32)
        br = br.at[:N_GROUPS, 0].set(b_group[l]).at[N_GROUPS:N_GROUPS + N_EXPERTS, 0].set(b_expert[l])

        u, lat, mu, mlat = _in_proj(h, meta, mix_norm_g[l][None, :], w_in_p, pool_w, _tile(n, 256))
        mlat_p = jnp.pad(mlat, ((0, meta_rows - N_META), (0, 0)))
        qt, k, vt, mk, mvt = _qkv(lat, tab_k, tab_qt, mlat_p, mtab_k, q_lat_norm_g[l][None, :],
                                  kv_lat_norm_g[l][None, :], gqc, gkh, wqt, wk, wvt,
                                  batch, seq, n_heads, q_rank, _tile(seq, 256))
        y_mla = _attention(qt, k, vt, mk, mvt, _tile(seq, ATTN_TQ)).reshape(n, n_heads * V_HEAD)
        h1 = _out_proj(u, mu, y_mla, h, w_pool[l].astype(BF16), pool_scale[l][None, :],
                       w_out[l].astype(BF16), seq, _tile(seq, 256), _tile(d, 2048))

        xp, ri, rf, cnt = _router(h1, ffn_norm_g[l][None, :], wr_t.astype(BF16), br, _tile(n, 512))
        counts = cnt[:, 0].astype(jnp.int32)
        padded = (counts + tm_exp - 1) // tm_exp * tm_exp
        pend = jnp.cumsum(padded)
        pstart = pend - padded
        n_used = (pend[-1] // tm_exp).astype(jnp.int32)
        blk_start = jnp.arange(n_blocks, dtype=jnp.int32) * tm_exp
        blk_expert = jnp.sum((pend[None, :] <= blk_start[:, None]).astype(jnp.int32), axis=1)
        blk_expert = jnp.minimum(blk_expert, N_EXPERTS - 1)
        npad = (pend[-1] - 2 * n).astype(jnp.int32)

        dest = _dest(ri, pstart.astype(F32)[:, None], _tile(n, 512))
        d1 = dest[0]
        d2 = dest[1]
        xs = _dispatch(d1, d2, counts, pstart.astype(jnp.int32), padded.astype(jnp.int32),
                       npad[None], n_used[None], xp, n_rows, _tile(n, 1024), tm_exp)
        ys = _experts(blk_expert, n_used[None], xs, w_gate[l], w_up[l], w_down[l], tm_exp)
        h = _combine(d1, d2, h1, rf, ys, _tile(n, 256))
        if l + 1 < depth:
            raise NotImplementedError("meta-token outputs are only produced for a single layer")
    return h.reshape(batch, seq, d)
```

```python
import functools

import jax
import jax.numpy as jnp
from jax import lax
from jax.experimental import pallas as pl
from jax.experimental.pallas import tpu as pltpu

F32 = jnp.float32
BF16 = jnp.bfloat16

N_META = 16
POOL_WINDOWS = (2, 4, 8, 16)
QK_NOPE = 128
QK_ROPE = 64
QK_HEAD = QK_NOPE + QK_ROPE
V_HEAD = 128
V_AUG = V_HEAD + 16
LOG2E = 1.4426950408889634
HEAD_PAD = 256
KV_LORA = 512
ROPE_THETA = 10000.0
N_GROUPS = 8
EXPERTS_PER_GROUP = 8
N_EXPERTS = N_GROUPS * EXPERTS_PER_GROUP
EPS = 1e-6
ROUTER_ROWS = 128
NEG_BIG = -1e30

VMEM_LIMIT = 56 * 1024 * 1024


def _cparams(sem, vmem=VMEM_LIMIT):
    return pltpu.CompilerParams(dimension_semantics=sem, vmem_limit_bytes=vmem)


def _rms(xf, g, n):
    ss = jnp.sum(xf * xf, axis=-1, keepdims=True)
    return xf * lax.rsqrt(ss * (1.0 / n) + EPS) * g


def _in_proj_kernel(x_ref, meta_ref, g_ref, w_ref, u_ref, lat_ref, mu_ref, mlat_ref, *, n_u, d):
    def proj(xf):
        xn = _rms(xf, g_ref[...], d).astype(BF16)
        return jnp.dot(xn, w_ref[...], preferred_element_type=F32)

    p = proj(x_ref[...])
    u_ref[...] = p[:, :n_u]
    lat_ref[...] = p[:, n_u:]

    @pl.when(pl.program_id(0) == 0)
    def _():
        pm = proj(meta_ref[...])
        mu_ref[...] = pm[:, :n_u]
        mlat_ref[...] = pm[:, n_u:]


def _in_proj(x2d, meta, g, w_p, n_u, tm):
    n, d = x2d.shape
    n_out = w_p.shape[1]
    n_lat = n_out - n_u
    return pl.pallas_call(
        functools.partial(_in_proj_kernel, n_u=n_u, d=d),
        grid=(n // tm,),
        in_specs=[
            pl.BlockSpec((tm, d), lambda i: (i, 0)),
            pl.BlockSpec((N_META, d), lambda i: (0, 0)),
            pl.BlockSpec((1, d), lambda i: (0, 0)),
            pl.BlockSpec((d, n_out), lambda i: (0, 0), pipeline_mode=pl.Buffered(1)),
        ],
        out_specs=[
            pl.BlockSpec((tm, n_u), lambda i: (i, 0)),
            pl.BlockSpec((tm, n_lat), lambda i: (i, 0)),
            pl.BlockSpec((N_META, n_u), lambda i: (0, 0)),
            pl.BlockSpec((N_META, n_lat), lambda i: (0, 0)),
        ],
        out_shape=[
            jax.ShapeDtypeStruct((n, n_u), F32),
            jax.ShapeDtypeStruct((n, n_lat), F32),
            jax.ShapeDtypeStruct((N_META, n_u), F32),
            jax.ShapeDtypeStruct((N_META, n_lat), F32),
        ],
        compiler_params=_cparams(("arbitrary",)),
        name="in_proj",
    )(x2d, meta, g, w_p)


def _lanes(a, m):
    return jnp.concatenate([a] * (m // 128), axis=1) if m > 128 else a


def _ones_rows(r, m):
    return jnp.where(lax.broadcasted_iota(jnp.int32, (r, m), 0) == 0, 1.0, 0.0).astype(BF16)


def _kv_rows(lat, tab, gkvl_ref, gkh_ref, wk_ref, wvt_ref, store_k, store_vt, *, n_heads, q_rank):
    kvl = lat[:, q_rank:q_rank + KV_LORA]
    kd = lat[:, q_rank + KV_LORA:q_rank + KV_LORA + 128]
    kr = lat[:, q_rank + KV_LORA + 128:q_rank + KV_LORA + 256]
    kvn = _rms(kvl, gkvl_ref[...], KV_LORA)
    kvn_b = kvn.astype(BF16)
    kvn_t = kvn.T.astype(BF16)
    gk0 = gkh_ref[:, :128]
    k_rope = kd * gkh_ref[:, 128:256] * tab[:, :128] + kr * gkh_ref[:, 256:384] * tab[:, 128:256]
    ss_kr = 0.5 * jnp.sum(kd * kd, axis=-1, keepdims=True)
    inv_n = 1.0 / QK_HEAD
    for p in range(n_heads // 2):
        kk = jnp.dot(kvn_b, wk_ref[:, p * 256:(p + 1) * 256], preferred_element_type=F32)
        vv = jnp.dot(wvt_ref[p * 256:(p + 1) * 256, :], kvn_t, preferred_element_type=F32)
        for t in range(2):
            kn = kk[:, t * 128:(t + 1) * 128]
            ssk = jnp.sum(kn * kn, axis=-1, keepdims=True) + ss_kr
            rk = lax.rsqrt(ssk * inv_n + EPS)
            store_k(2 * p + t, jnp.concatenate([kn * rk * gk0, k_rope * rk], axis=-1).astype(BF16))
            store_vt(2 * p + t, vv[t * 128:(t + 1) * 128].astype(BF16))


def _qkv_kernel(lat_ref, tab_ref, tqt_ref, mlat_ref, mtab_ref, gql_ref, gkvl_ref, gqc_ref, gkh_ref,
                wqt_ref, wk_ref, wvt_ref, qt_ref, k_ref, vt_ref, mk_ref, mvt_ref, *, n_heads, q_rank):
    lat = lat_ref[...]
    m = lat.shape[0]

    def sk(h, val):
        k_ref[h] = val

    def svt(h, val):
        vt_ref[h, :V_HEAD, :] = val
        vt_ref[h, V_HEAD:, :] = _ones_rows(V_AUG - V_HEAD, val.shape[1])

    _kv_rows(lat, tab_ref[...], gkvl_ref, gkh_ref, wk_ref, wvt_ref, sk, svt, n_heads=n_heads, q_rank=q_rank)

    scale = QK_HEAD ** -0.5 * LOG2E
    qn_t = _rms(lat[:, :q_rank], gql_ref[...], q_rank).T.astype(BF16)
    gq0 = _lanes(gqc_ref[:128, :], m) * scale
    gq1 = _lanes(gqc_ref[128:, :], m) * tqt_ref[...] * scale
    inv_n = 1.0 / QK_HEAD
    for h in range(n_heads):
        xh = jnp.dot(wqt_ref[h * HEAD_PAD:(h + 1) * HEAD_PAD, :], qn_t, preferred_element_type=F32)
        x0 = xh[:128]
        x1 = xh[128:]
        ss = jnp.sum(x0 * x0, axis=0, keepdims=True) + 0.5 * jnp.sum(x1 * x1, axis=0, keepdims=True)
        rinv = lax.rsqrt(ss * inv_n + EPS)
        qt_ref[h] = jnp.concatenate([x0 * rinv * gq0, x1 * rinv * gq1], axis=0).astype(BF16)

    @pl.when((pl.program_id(0) == 0) & (pl.program_id(1) == 0))
    def _():
        def smk(h, val):
            mk_ref[h] = val

        def smvt(h, val):
            mvt_ref[h, :V_HEAD, :] = val
            mvt_ref[h, V_HEAD:, :] = _ones_rows(V_AUG - V_HEAD, val.shape[1])

        _kv_rows(mlat_ref[...], mtab_ref[...], gkvl_ref, gkh_ref, wk_ref, wvt_ref, smk, smvt,
                 n_heads=n_heads, q_rank=q_rank)


def _qkv(lat, tab, tqt, mlat, mtab, gql, gkvl, gqc, gkh, wqt, wk, wvt, batch, seq, n_heads, q_rank, tm):
    n_lat = lat.shape[1]
    tpb = seq // tm
    mp = mlat.shape[0]
    const = lambda b, i: (0, 0)
    return pl.pallas_call(
        functools.partial(_qkv_kernel, n_heads=n_heads, q_rank=q_rank),
        grid=(batch, tpb),
        in_specs=[
            pl.BlockSpec((tm, n_lat), lambda b, i: (b * tpb + i, 0)),
            pl.BlockSpec((tm, 256), lambda b, i: (i, 0)),
            pl.BlockSpec((128, tm), lambda b, i: (0, i)),
            pl.BlockSpec((mp, n_lat), const),
            pl.BlockSpec((mp, 256), const),
            pl.BlockSpec((1, q_rank), const),
            pl.BlockSpec((1, KV_LORA), const),
            pl.BlockSpec((256, 128), const),
            pl.BlockSpec((1, 384), const),
            pl.BlockSpec(wqt.shape, const, pipeline_mode=pl.Buffered(1)),
            pl.BlockSpec(wk.shape, const, pipeline_mode=pl.Buffered(1)),
            pl.BlockSpec(wvt.shape, const, pipeline_mode=pl.Buffered(1)),
        ],
        out_specs=[
            pl.BlockSpec((None, n_heads, HEAD_PAD, tm), lambda b, i: (b, 0, 0, i)),
            pl.BlockSpec((None, n_heads, tm, HEAD_PAD), lambda b, i: (b, 0, i, 0)),
            pl.BlockSpec((None, n_heads, None, V_AUG, tm), lambda b, i: (b, 0, i, 0, 0)),
            pl.BlockSpec((n_heads, mp, HEAD_PAD), lambda b, i: (0, 0, 0)),
            pl.BlockSpec((n_heads, V_AUG, mp), lambda b, i: (0, 0, 0)),
        ],
        out_shape=[
            jax.ShapeDtypeStruct((batch, n_heads, HEAD_PAD, seq), BF16),
            jax.ShapeDtypeStruct((batch, n_heads, seq, HEAD_PAD), BF16),
            jax.ShapeDtypeStruct((batch, n_heads, tpb, V_AUG, tm), BF16),
            jax.ShapeDtypeStruct((n_heads, mp, HEAD_PAD), BF16),
            jax.ShapeDtypeStruct((n_heads, V_AUG, mp), BF16),
        ],
        compiler_params=_cparams(("arbitrary", "arbitrary")),
        name="qkv",
    )(lat, tab, tqt, mlat, mtab, gql, gkvl, gqc, gkh, wqt, wk, wvt)


ATTN_HEADS = 2
ATTN_STRIP = 256
ATTN_TQ = 2048
ATTN_TK = 1024
ATTN_SM_LAG = 1
ATTN_PV_LAG = 3


def _attn_kernel(qt_ref, k_ref, vt_ref, mk_ref, mvt_ref, o_ref, acc_ref, *, tq, tv):
    qi = pl.program_id(2)
    nh = qt_ref.shape[0]
    sw = ATTN_STRIP
    chains = [(hh, x) for x in range(tq // sw) for hh in range(nh)]
    qts = [qt_ref[hh, :, x * sw:(x + 1) * sw] for hh, x in chains]

    carry = []
    for ci, (hh, x) in enumerate(chains):
        s0 = jnp.dot(mk_ref[hh], qts[ci], preferred_element_type=F32)
        s0 = jnp.where(lax.broadcasted_iota(jnp.int32, s0.shape, 0) < N_META, s0, NEG_BIG)
        m0 = jnp.max(s0, axis=0, keepdims=True)
        carry.append(m0)
        acc_ref[hh, :, x * sw:(x + 1) * sw] = jnp.dot(mvt_ref[hh], jnp.exp2(s0 - m0).astype(BF16),
                                                     preferred_element_type=F32)

    tk = ATTN_TK
    kpq = tq // tk

    def n_keys(x, diag):
        if diag is None:
            return tk
        return max(0, min(tk, (x + 1) * sw - diag * tk))

    def scores(ci, j, diag):
        hh, x = chains[ci]
        start = pl.multiple_of(j * tk, tk)
        s = jnp.dot(k_ref[hh, pl.ds(start, n_keys(x, diag)), :], qts[ci],
                    preferred_element_type=F32)
        if diag is not None and (diag + 1) * tk - 1 > x * sw:
            key = lax.broadcasted_iota(jnp.int32, s.shape, 0) + diag * tk
            qry = lax.broadcasted_iota(jnp.int32, s.shape, 1) + x * sw
            s = jnp.where(key <= qry, s, NEG_BIG)
        return s

    def softmax(s, m_prev):
        m_new = jnp.maximum(m_prev, jnp.max(s, axis=0, keepdims=True))
        return m_new, jnp.exp2(m_prev - m_new), jnp.exp2((s - m_new).astype(BF16))

    def values(ci, j, alpha, pb):
        hh, x = chains[ci]
        pv = jnp.dot(vt_ref[hh, j * (tk // tv)], pb[:tv], preferred_element_type=F32)
        for c in range(1, pb.shape[0] // tv):
            pv += jnp.dot(vt_ref[hh, j * (tk // tv) + c], pb[c * tv:(c + 1) * tv],
                          preferred_element_type=F32)
        acc_ref[hh, :, x * sw:(x + 1) * sw] = alpha * acc_ref[hh, :, x * sw:(x + 1) * sw] + pv

    def step(j, c, diag=None):
        live = [ci for ci, (hh, x) in enumerate(chains) if n_keys(x, diag) > 0]
        n = len(live)
        out = list(c)
        s_vals = {}
        sm_vals = {}
        lag = ATTN_PV_LAG
        for t in range(n + lag):
            if t < n:
                s_vals[t] = scores(live[t], j, diag)
            if 0 <= t - ATTN_SM_LAG < n:
                ts = t - ATTN_SM_LAG
                ci = live[ts]
                out[ci], alpha, pb = softmax(s_vals.pop(ts), c[ci])
                sm_vals[ts] = (alpha, pb)
            if 0 <= t - lag < n:
                values(live[t - lag], j, *sm_vals.pop(t - lag))
        return tuple(out)

    c = lax.fori_loop(0, qi * kpq, lambda j, c: step(j, c), tuple(carry))
    for diag in range(kpq):
        c = step(qi * kpq + diag, c, diag)
    for hh, x in chains:
        acc = acc_ref[hh, :, x * sw:(x + 1) * sw]
        o_ref[x * sw:(x + 1) * sw, hh * V_HEAD:(hh + 1) * V_HEAD] = (
            acc[:V_HEAD] / acc[V_HEAD:V_HEAD + 1]).T.astype(o_ref.dtype)


def _attention(qt, k, vt, mk, mvt, tq):
    batch, n_heads, seq, _ = k.shape
    n_chunks, tv = vt.shape[2], vt.shape[4]
    mp = mk.shape[1]
    nh = ATTN_HEADS
    return pl.pallas_call(
        functools.partial(_attn_kernel, tq=tq, tv=tv),
        grid=(batch, n_heads // nh, seq // tq),
        in_specs=[
            pl.BlockSpec((None, nh, HEAD_PAD, tq), lambda b, h, i: (b, h, 0, i)),
            pl.BlockSpec((None, nh, seq, HEAD_PAD), lambda b, h, i: (b, h, 0, 0)),
            pl.BlockSpec((None, nh, n_chunks, V_AUG, tv), lambda b, h, i: (b, h, 0, 0, 0)),
            pl.BlockSpec((nh, mp, HEAD_PAD), lambda b, h, i: (h, 0, 0)),
            pl.BlockSpec((nh, V_AUG, mp), lambda b, h, i: (h, 0, 0)),
        ],
        out_specs=pl.BlockSpec((None, tq, nh * V_HEAD), lambda b, h, i: (b, i, h)),
        out_shape=jax.ShapeDtypeStruct((batch, seq, n_heads * V_HEAD), BF16),
        scratch_shapes=[pltpu.VMEM((nh, V_AUG, tq), F32)],
        compiler_params=_cparams(("arbitrary", "arbitrary", "arbitrary")),
        name="attention",
    )(qt, k, vt, mk, mvt)


def _out_proj_kernel(u_ref, halo_ref, mu_ref, y_ref, x_ref, wp_ref, ps_ref, wo_ref, o_ref,
                     *, tiles_per_batch, group):
    i = pl.program_id(1)
    first = (i % tiles_per_batch) == 0
    halo = jnp.where(first, mu_ref[...], halo_ref[...])
    ucat = jnp.concatenate([halo, u_ref[...]], axis=0)
    ys = []
    for gi, w in enumerate(POOL_WINDOWS):
        ug = ucat[:, gi * group:(gi + 1) * group]
        s = ug
        shift = 1
        while shift < w:
            s = s + pltpu.roll(s, shift, 0)
            shift *= 2
        diff = (s * (1.0 / w) - ug)[N_META:]
        yg = jnp.dot(diff.astype(BF16), wp_ref[gi], preferred_element_type=F32)
        ys.append((yg * ps_ref[:, gi * group:(gi + 1) * group]).astype(BF16))
    ycat = jnp.concatenate(ys + [y_ref[...]], axis=-1)
    o_ref[...] = x_ref[...] + jnp.dot(ycat, wo_ref[...], preferred_element_type=F32)


def _out_proj(u, mu, y_mla, x2d, w_pool, pool_scale, w_out, seq, tm, tn):
    n, d = x2d.shape
    pool_w = u.shape[1]
    mla_w = y_mla.shape[1]
    group = pool_w // len(POOL_WINDOWS)
    hb = tm // N_META
    return pl.pallas_call(
        functools.partial(_out_proj_kernel, tiles_per_batch=seq // tm, group=group),
        grid=(d // tn, n // tm),
        in_specs=[
            pl.BlockSpec((tm, pool_w), lambda j, i: (i, 0)),
            pl.BlockSpec((N_META, pool_w), lambda j, i: (jnp.maximum(i * hb - 1, 0), 0)),
            pl.BlockSpec((N_META, pool_w), lambda j, i: (0, 0)),
            pl.BlockSpec((tm, mla_w), lambda j, i: (i, 0)),
            pl.BlockSpec((tm, tn), lambda j, i: (i, j)),
            pl.BlockSpec(w_pool.shape, lambda j, i: (0, 0, 0)),
            pl.BlockSpec((1, pool_w), lambda j, i: (0, 0)),
            pl.BlockSpec((d, tn), lambda j, i: (0, j)),
        ],
        out_specs=pl.BlockSpec((tm, tn), lambda j, i: (i, j)),
        out_shape=jax.ShapeDtypeStruct((n, d), F32),
        compiler_params=_cparams(("arbitrary", "arbitrary")),
        name="out_proj",
    )(u, u, mu, y_mla, x2d, w_pool, pool_scale, w_out)


def _pack_bf16_pair(lo, hi):
    lo_b = lax.bitcast_convert_type(lo.astype(BF16).astype(F32), jnp.uint32)
    hi_b = lax.bitcast_convert_type(hi.astype(BF16).astype(F32), jnp.uint32)
    return lax.shift_right_logical(lo_b, jnp.uint32(16)) | hi_b


def _unpack_bf16_pair(w):
    lo = lax.bitcast_convert_type(lax.shift_left(w, jnp.uint32(16)), F32)
    hi = lax.bitcast_convert_type(w & jnp.uint32(0xFFFF0000), F32)
    return lo, hi


def _first_argmax(vals, sub, n):
    mx = jnp.max(vals, axis=0, keepdims=True)
    idx = jnp.min(jnp.where(vals == mx, sub, n), axis=0, keepdims=True)
    return mx, idx


def _router_kernel(h_ref, g_ref, wr_ref, br_ref, xp_ref, ri_ref, rf_ref, cnt_ref, *, d):
    xn = _rms(h_ref[...], g_ref[...], d)
    half = d // 2
    xp_ref[...] = _pack_bf16_pair(xn[:, :half], xn[:, half:])
    logits = lax.dot_general(wr_ref[...], xn.astype(BF16), (((1,), (1,)), ((), ())),
                             preferred_element_type=F32) + br_ref[...]
    tm = logits.shape[1]
    sub = lax.broadcasted_iota(jnp.int32, (N_GROUPS, tm), 0)
    lg = logits[:N_GROUPS]
    gmax, g_idx = _first_argmax(lg, sub, N_GROUPS)
    g_p = 1.0 / jnp.sum(jnp.exp(lg - gmax), axis=0, keepdims=True)
    in_group = jnp.zeros((EXPERTS_PER_GROUP, tm), F32)
    for g in range(N_GROUPS):
        lo = N_GROUPS + g * EXPERTS_PER_GROUP
        in_group = jnp.where(g_idx == g, logits[lo:lo + EXPERTS_PER_GROUP], in_group)
    m1, i1 = _first_argmax(in_group, sub, EXPERTS_PER_GROUP)
    rest = jnp.where(sub == i1, -jnp.inf, in_group)
    m2, i2 = _first_argmax(rest, sub, EXPERTS_PER_GROUP)
    e = jnp.exp(m2 - m1)
    w1 = g_p / (1.0 + e)
    w2 = g_p * e / (1.0 + e)
    e1 = g_idx * EXPERTS_PER_GROUP + i1
    e2 = g_idx * EXPERTS_PER_GROUP + i2
    ri_ref[...] = jnp.where(sub == 0, e1, jnp.where(sub == 1, e2, 0))
    rf_ref[:, :128] = jnp.broadcast_to(w1, (128, tm)).T
    rf_ref[:, 128:] = jnp.broadcast_to(w2, (128, tm)).T
    sub_e = lax.broadcasted_iota(jnp.int32, (N_EXPERTS, tm), 0)
    hit = jnp.where((sub_e == e1) | (sub_e == e2), 1.0, 0.0)
    cnt = jnp.sum(hit, axis=1, keepdims=True)

    @pl.when(pl.program_id(0) == 0)
    def _():
        cnt_ref[...] = jnp.zeros_like(cnt_ref)

    cnt_ref[...] += jnp.broadcast_to(cnt, cnt_ref.shape)


def _router(h1, g, wr_t, br, tm):
    n, d = h1.shape
    return pl.pallas_call(
        functools.partial(_router_kernel, d=d),
        grid=(n // tm,),
        in_specs=[
            pl.BlockSpec((tm, d), lambda i: (i, 0)),
            pl.BlockSpec((1, d), lambda i: (0, 0)),
            pl.BlockSpec((ROUTER_ROWS, d), lambda i: (0, 0)),
            pl.BlockSpec((ROUTER_ROWS, 1), lambda i: (0, 0)),
        ],
        out_specs=[
            pl.BlockSpec((tm, d // 2), lambda i: (i, 0)),
            pl.BlockSpec((8, tm), lambda i: (0, i)),
            pl.BlockSpec((tm, 256), lambda i: (i, 0)),
            pl.BlockSpec((N_EXPERTS, 128), lambda i: (0, 0)),
        ],
        out_shape=[
            jax.ShapeDtypeStruct((n, d // 2), jnp.uint32),
            jax.ShapeDtypeStruct((8, n), jnp.int32),
            jax.ShapeDtypeStruct((n, 256), F32),
            jax.ShapeDtypeStruct((N_EXPERTS, 128), F32),
        ],
        compiler_params=_cparams(("arbitrary",)),
        name="router",
    )(h1, g, wr_t, br)


def _dest_kernel(ri_ref, pstart_ref, tri_ref, d_ref, carry_ref):
    @pl.when(pl.program_id(0) == 0)
    def _():
        carry_ref[...] = jnp.zeros_like(carry_ref)

    tn = ri_ref.shape[1]
    e1 = ri_ref[0:1, :]
    e2 = ri_ref[1:2, :]
    sub = lax.broadcasted_iota(jnp.int32, (N_EXPERTS, tn), 0)
    oh1 = sub == e1
    oh2 = sub == e2
    oh = jnp.where(oh1 | oh2, 1.0, 0.0)
    c = jnp.dot(oh.astype(BF16), tri_ref[...], preferred_element_type=F32)
    val = c + (pstart_ref[...] + carry_ref[:, :1] - 1.0)
    d1 = jnp.sum(jnp.where(oh1, val, 0.0), axis=0, keepdims=True)
    d2 = jnp.sum(jnp.where(oh2, val, 0.0), axis=0, keepdims=True)
    sub8 = lax.broadcasted_iota(jnp.int32, (8, tn), 0)
    d_ref[...] = jnp.where(sub8 == 0, d1, jnp.where(sub8 == 1, d2, 0.0)).astype(jnp.int32)
    carry_ref[...] += jnp.broadcast_to(jnp.sum(oh, axis=1, keepdims=True), carry_ref.shape)


def _dest(ri, pstart_col, tn):
    n = ri.shape[1]
    tri = (jnp.arange(tn)[:, None] <= jnp.arange(tn)[None, :]).astype(BF16)
    return pl.pallas_call(
        _dest_kernel,
        grid=(n // tn,),
        in_specs=[
            pl.BlockSpec((8, tn), lambda i: (0, i)),
            pl.BlockSpec((N_EXPERTS, 1), lambda i: (0, 0)),
            pl.BlockSpec((tn, tn), lambda i: (0, 0)),
        ],
        out_specs=pl.BlockSpec((8, tn), lambda i: (0, i)),
        out_shape=jax.ShapeDtypeStruct((8, n), jnp.int32),
        scratch_shapes=[pltpu.VMEM((N_EXPERTS, 128), F32)],
        compiler_params=_cparams(("arbitrary",)),
        name="dest",
    )(ri, pstart_col, tri)


def _dispatch_kernel(d1_ref, d2_ref, cnt_ref, pstart_ref, padded_ref, npad_ref, nused_ref,
                     xp_ref, xs_ref, zero_ref, sem, zsem, bsem, *, tm, tb, n_blocks):
    i = pl.program_id(0)
    base = i * tm

    def row_copy(r, dst):
        return pltpu.make_async_copy(xp_ref.at[pl.ds(r, 1), :], xs_ref.at[pl.ds(dst, 1), :], sem)

    def zero_row_copy(dst):
        return pltpu.make_async_copy(zero_ref.at[pl.ds(0, 1), :], xs_ref.at[pl.ds(dst, 1), :], zsem)

    def zero_block_copy(blk):
        start = pl.multiple_of(blk * tb, tb)
        return pltpu.make_async_copy(zero_ref, xs_ref.at[pl.ds(start, tb), :], bsem)

    def issue(r, c):
        row_copy(r, d1_ref[base + r]).start()
        row_copy(r, d2_ref[base + r]).start()
        return c

    lax.fori_loop(0, tm, issue, 0, unroll=4)

    @pl.when(i == 0)
    def _():
        zero_ref[...] = jnp.zeros_like(zero_ref)

        def per_expert(e, c):
            lo = pstart_ref[e] + cnt_ref[e]
            hi = pstart_ref[e] + padded_ref[e]

            def per_row(p, c2):
                zero_row_copy(p).start()
                return c2

            lax.fori_loop(lo, hi, per_row, 0)
            return c

        lax.fori_loop(0, N_EXPERTS, per_expert, 0)

        def per_block(b, c):
            zero_block_copy(b).start()
            return c

        lax.fori_loop(nused_ref[0], n_blocks, per_block, 0)

        def zwait(p, c):
            zero_row_copy(0).wait()
            return c

        lax.fori_loop(0, npad_ref[0], zwait, 0)

        def bwait(b, c):
            zero_block_copy(0).wait()
            return c

        lax.fori_loop(nused_ref[0], n_blocks, bwait, 0)

    for _ in range(2):
        pltpu.make_async_copy(xp_ref, xs_ref.at[pl.ds(0, tm), :], sem).wait()


def _dispatch(d1, d2, cnt, pstart, padded, npad, n_used, xp, n_rows, tm, tb):
    n, half = xp.shape
    grid_spec = pltpu.PrefetchScalarGridSpec(
        num_scalar_prefetch=7,
        grid=(n // tm,),
        in_specs=[pl.BlockSpec((tm, half), lambda i, *_: (i, 0))],
        out_specs=pl.BlockSpec(memory_space=pl.ANY),
        scratch_shapes=[
            pltpu.VMEM((tb, half), jnp.uint32),
            pltpu.SemaphoreType.DMA,
            pltpu.SemaphoreType.DMA,
            pltpu.SemaphoreType.DMA,
        ],
    )
    return pl.pallas_call(
        functools.partial(_dispatch_kernel, tm=tm, tb=tb, n_blocks=n_rows // tb),
        grid_spec=grid_spec,
        out_shape=jax.ShapeDtypeStruct((n_rows, half), jnp.uint32),
        compiler_params=pltpu.CompilerParams(
            dimension_semantics=("arbitrary",), vmem_limit_bytes=VMEM_LIMIT, has_side_effects=True),
        name="dispatch",
    )(d1, d2, cnt, pstart, padded, npad, n_used, xp)


def _experts_kernel(be_ref, nused_ref, slot_ref, nexte_ref, xs_ref, wg_hbm, wu_hbm, wd_hbm, ys_ref,
                    wg_ref, wu_ref, wd_ref, gu_sem, wd_sem):
    i = pl.program_id(0)
    active = i < nused_ref[0]
    fresh = (i == 0) | (be_ref[i] != be_ref[jnp.maximum(i - 1, 0)])
    slot = slot_ref[i]

    def gu_copies(e, s):
        return (pltpu.make_async_copy(wg_hbm.at[e], wg_ref.at[s], gu_sem.at[s]),
                pltpu.make_async_copy(wu_hbm.at[e], wu_ref.at[s], gu_sem.at[s]))

    def wd_copy():
        return pltpu.make_async_copy(wd_hbm.at[be_ref[i]], wd_ref, wd_sem)

    @pl.when(active & (i == 0))
    def _():
        for cp in gu_copies(be_ref[0], 0):
            cp.start()

    @pl.when(active & fresh)
    def _():
        wd_copy().start()
        for cp in gu_copies(be_ref[i], slot):
            cp.wait()

    @pl.when(active)
    def _():
        lo, hi = _unpack_bf16_pair(xs_ref[...])
        half = lo.shape[1]
        g = (jnp.dot(lo, wg_ref[slot, :half], preferred_element_type=F32)
             + jnp.dot(hi, wg_ref[slot, half:], preferred_element_type=F32))
        u = (jnp.dot(lo, wu_ref[slot, :half], preferred_element_type=F32)
             + jnp.dot(hi, wu_ref[slot, half:], preferred_element_type=F32))
        hb = g / (1.0 + jnp.exp(-g)) * u

        @pl.when(fresh)
        def _():
            wd_copy().wait()

            @pl.when(nexte_ref[i] >= 0)
            def _():
                for cp in gu_copies(nexte_ref[i], 1 - slot):
                    cp.start()

        y = jnp.dot(hb, wd_ref[...], preferred_element_type=F32)
        ys_ref[...] = _pack_bf16_pair(y[:, :half], y[:, half:])

    @pl.when(i >= nused_ref[0])
    def _():
        ys_ref[...] = jnp.zeros_like(ys_ref)


def _experts(blk_expert, n_used, xs, wg, wu, wd, tm):
    n_rows, half = xs.shape
    _, d, f = wg.shape
    n_blocks = n_rows // tm

    idx = jnp.arange(n_blocks, dtype=jnp.int32)
    fresh = jnp.concatenate([jnp.ones((1,), bool), blk_expert[1:] != blk_expert[:-1]]) & (idx < n_used[0])
    slot = ((jnp.cumsum(fresh.astype(jnp.int32)) - 1) % 2).astype(jnp.int32)
    later = jnp.where(fresh, idx, n_blocks)
    nxt = lax.cummin(jnp.concatenate([later[1:], jnp.full((1,), n_blocks, jnp.int32)]), reverse=True)
    next_e = jnp.where(nxt < n_blocks, blk_expert[jnp.minimum(nxt, n_blocks - 1)], -1).astype(jnp.int32)

    def row_map(i, *_):
        return (i, 0)

    grid_spec = pltpu.PrefetchScalarGridSpec(
        num_scalar_prefetch=4,
        grid=(n_blocks,),
        in_specs=[
            pl.BlockSpec((tm, half), row_map),
            pl.BlockSpec(memory_space=pl.ANY),
            pl.BlockSpec(memory_space=pl.ANY),
            pl.BlockSpec(memory_space=pl.ANY),
        ],
        out_specs=pl.BlockSpec((tm, half), row_map),
        scratch_shapes=[
            pltpu.VMEM((2, d, f), F32),
            pltpu.VMEM((2, d, f), F32),
            pltpu.VMEM((f, d), F32),
            pltpu.SemaphoreType.DMA((2,)),
            pltpu.SemaphoreType.DMA,
        ],
    )
    return pl.pallas_call(
        _experts_kernel,
        grid_spec=grid_spec,
        out_shape=jax.ShapeDtypeStruct((n_rows, half), jnp.uint32),
        compiler_params=_cparams(("arbitrary",)),
        name="experts",
    )(blk_expert, n_used, slot, next_e, xs, wg, wu, wd)


def _combine_kernel(d1_ref, d2_ref, h_ref, w_ref, ys_ref, o_ref, a_ref, b_ref, sem, *, tm, n_steps):
    i = pl.program_id(0)
    slot = i % 2

    def gather(step, s):
        base = step * tm

        def row_copy(src, r, buf):
            return pltpu.make_async_copy(ys_ref.at[pl.ds(src, 1), :], buf.at[s, pl.ds(r, 1), :], sem.at[s])

        def issue(r, c):
            row_copy(d1_ref[base + r], r, a_ref).start()
            row_copy(d2_ref[base + r], r, b_ref).start()
            return c

        lax.fori_loop(0, tm, issue, 0, unroll=4)

    @pl.when(i == 0)
    def _():
        gather(0, 0)

    @pl.when(i + 1 < n_steps)
    def _():
        gather(i + 1, 1 - slot)

    for buf in (a_ref, b_ref):
        pltpu.make_async_copy(ys_ref.at[pl.ds(0, tm), :], buf.at[slot], sem.at[slot]).wait()

    half = a_ref.shape[2]
    rows = 8

    def slab(r, c):
        r0 = pl.multiple_of(r * rows, rows)
        a_lo, a_hi = _unpack_bf16_pair(a_ref[slot, pl.ds(r0, rows), :])
        b_lo, b_hi = _unpack_bf16_pair(b_ref[slot, pl.ds(r0, rows), :])
        w1 = _lanes(w_ref[pl.ds(r0, rows), :128], half)
        w2 = _lanes(w_ref[pl.ds(r0, rows), 128:], half)
        o_ref[pl.ds(r0, rows), :half] = h_ref[pl.ds(r0, rows), :half] + (w1 * a_lo + w2 * b_lo)
        o_ref[pl.ds(r0, rows), half:] = h_ref[pl.ds(r0, rows), half:] + (w1 * a_hi + w2 * b_hi)
        return c

    lax.fori_loop(0, tm // rows, slab, 0, unroll=2)


def _combine(d1, d2, h1, wcol, ys, tm):
    n, d = h1.shape
    half = d // 2
    grid_spec = pltpu.PrefetchScalarGridSpec(
        num_scalar_prefetch=2,
        grid=(n // tm,),
        in_specs=[
            pl.BlockSpec((tm, d), lambda i, *_: (i, 0)),
            pl.BlockSpec((tm, 256), lambda i, *_: (i, 0)),
            pl.BlockSpec(memory_space=pl.ANY),
        ],
        out_specs=pl.BlockSpec((tm, d), lambda i, *_: (i, 0)),
        scratch_shapes=[
            pltpu.VMEM((2, tm, half), jnp.uint32),
            pltpu.VMEM((2, tm, half), jnp.uint32),
            pltpu.SemaphoreType.DMA((2,)),
        ],
    )
    return pl.pallas_call(
        functools.partial(_combine_kernel, tm=tm, n_steps=n // tm),
        grid_spec=grid_spec,
        out_shape=jax.ShapeDtypeStruct((n, d), F32),
        compiler_params=_cparams(("arbitrary",)),
        name="combine",
    )(d1, d2, h1, wcol, ys)


def _rope_tab(pos):
    inv = 1.0 / (ROPE_THETA ** (jnp.arange(0, QK_ROPE, 2, dtype=F32) / QK_ROPE))
    ang = pos.astype(F32)[:, None] * inv[None, :]
    c = jnp.cos(ang)
    s = jnp.sin(ang)
    cos64 = jnp.concatenate([c, c], axis=-1)
    sin64 = jnp.concatenate([-s, s], axis=-1)
    tab_q = jnp.concatenate([cos64, sin64], axis=-1)
    tab_k = jnp.concatenate([cos64, cos64, sin64, sin64], axis=-1)
    return tab_q, tab_k


def _swap_halves(a):
    h = a.shape[-1] // 2
    return jnp.concatenate([a[..., h:], a[..., :h]], axis=-1)


def _tile(n, pref):
    t = pref
    while n % t:
        t //= 2
    return t


def kernel(x, meta_tokens, mix_norm_g, w_in, q_lat_norm_g, w_uq, kv_lat_norm_g, w_ukv, q_head_norm_g,
           k_head_norm_g, w_pool, pool_scale, w_out, ffn_norm_g, w_group, b_group, w_expert, b_expert,
           w_gate, w_up, w_down):
    batch, seq, d = x.shape
    depth = w_in.shape[0]
    pool_w = pool_scale.shape[-1]
    q_rank = q_lat_norm_g.shape[-1]
    n_heads = w_uq.shape[-1] // QK_HEAD
    n = batch * seq
    f = w_gate.shape[-1]

    tab_q, tab_k = _rope_tab(N_META + jnp.arange(seq))
    tab_qt = tab_q.T
    meta_rows = 128
    _, mtab_k = _rope_tab(jnp.arange(meta_rows))
    tm_exp = _tile(n, 256)
    n_blocks = (2 * n + N_EXPERTS * (tm_exp - 1) + tm_exp - 1) // tm_exp
    n_rows = n_blocks * tm_exp

    h = x.reshape(n, d)
    meta = meta_tokens.astype(F32)
    for l in range(depth):
        k_r = w_in[l][:, pool_w + q_rank + KV_LORA:]
        k_rs = _swap_halves(k_r)
        w_in_p = jnp.concatenate([w_in[l][:, :pool_w + q_rank + KV_LORA], k_r, k_r, k_rs, k_rs],
                                 axis=-1).astype(BF16)
        wq = w_uq[l].reshape(q_rank, n_heads, QK_HEAD)
        wuq_p = jnp.concatenate([wq, _swap_halves(wq[..., QK_NOPE:])], axis=-1)
        wqt = wuq_p.reshape(q_rank, n_heads * HEAD_PAD).T.astype(BF16)
        wkv = w_ukv[l].reshape(KV_LORA, n_heads, QK_NOPE + V_HEAD)
        wk = wkv[..., :QK_NOPE].reshape(KV_LORA, n_heads * QK_NOPE).astype(BF16)
        wvt = wkv[..., QK_NOPE:].reshape(KV_LORA, n_heads * V_HEAD).T.astype(BF16)
        gq = q_head_norm_g[l]
        gk = k_head_norm_g[l]
        gqc = jnp.broadcast_to(jnp.concatenate([gq, _swap_halves(gq[QK_NOPE:])])[:, None], (HEAD_PAD, 128))
        gk_r = gk[QK_NOPE:]
        gkh = jnp.concatenate([gk[:QK_NOPE], gk_r, gk_r, _swap_halves(gk_r), _swap_halves(gk_r)])[None, :]
        wr_t = jnp.zeros((ROUTER_ROWS, d), F32)
        wr_t = wr_t.at[:N_GROUPS].set(w_group[l].T).at[N_GROUPS:N_GROUPS + N_EXPERTS].set(w_expert[l].T)
        br = jnp.zeros((ROUTER_ROWS, 1), F32)
        br = br.at[:N_GROUPS, 0].set(b_group[l]).at[N_GROUPS:N_GROUPS + N_EXPERTS, 0].set(b_expert[l])

        u, lat, mu, mlat = _in_proj(h, meta, mix_norm_g[l][None, :], w_in_p, pool_w, _tile(n, 256))
        mlat_p = jnp.pad(mlat, ((0, meta_rows - N_META), (0, 0)))
        qt, k, vt, mk, mvt = _qkv(lat, tab_k, tab_qt, mlat_p, mtab_k, q_lat_norm_g[l][None, :],
                                  kv_lat_norm_g[l][None, :], gqc, gkh, wqt, wk, wvt,
                                  batch, seq, n_heads, q_rank, _tile(seq, 256))
        y_mla = _attention(qt, k, vt, mk, mvt, _tile(seq, ATTN_TQ)).reshape(n, n_heads * V_HEAD)
        h1 = _out_proj(u, mu, y_mla, h, w_pool[l].astype(BF16), pool_scale[l][None, :],
                       w_out[l].astype(BF16), seq, _tile(seq, 256), _tile(d, 2048))

        xp, ri, rf, cnt = _router(h1, ffn_norm_g[l][None, :], wr_t.astype(BF16), br, _tile(n, 512))
        counts = cnt[:, 0].astype(jnp.int32)
        padded = (counts + tm_exp - 1) // tm_exp * tm_exp
        pend = jnp.cumsum(padded)
        pstart = pend - padded
        n_used = (pend[-1] // tm_exp).astype(jnp.int32)
        blk_start = jnp.arange(n_blocks, dtype=jnp.int32) * tm_exp
        blk_expert = jnp.sum((pend[None, :] <= blk_start[:, None]).astype(jnp.int32), axis=1)
        blk_expert = jnp.minimum(blk_expert, N_EXPERTS - 1)
        npad = (pend[-1] - 2 * n).astype(jnp.int32)

        dest = _dest(ri, pstart.astype(F32)[:, None], _tile(n, 512))
        d1 = dest[0]
        d2 = dest[1]
        xs = _dispatch(d1, d2, counts, pstart.astype(jnp.int32), padded.astype(jnp.int32),
                       npad[None], n_used[None], xp, n_rows, _tile(n, 1024), tm_exp)
        ys = _experts(blk_expert, n_used[None], xs, w_gate[l], w_up[l], w_down[l], tm_exp)
        h = _combine(d1, d2, h1, rf, ys, _tile(n, 256))
        if l + 1 < depth:
            raise NotImplementedError("meta-token outputs are only produced for a single layer")
    return h.reshape(batch, seq, d)
```

```python
import functools

import jax
import jax.numpy as jnp
from jax import lax
from jax.experimental import pallas as pl
from jax.experimental.pallas import tpu as pltpu

F32 = jnp.float32
BF16 = jnp.bfloat16

N_META = 16
POOL_WINDOWS = (2, 4, 8, 16)
QK_NOPE = 128
QK_ROPE = 64
QK_HEAD = QK_NOPE + QK_ROPE
V_HEAD = 128
V_AUG = V_HEAD + 16
LOG2E = 1.4426950408889634
HEAD_PAD = 256
KV_LORA = 512
ROPE_THETA = 10000.0
N_GROUPS = 8
EXPERTS_PER_GROUP = 8
N_EXPERTS = N_GROUPS * EXPERTS_PER_GROUP
EPS = 1e-6
ROUTER_ROWS = 128
NEG_BIG = -1e30

VMEM_LIMIT = 56 * 1024 * 1024


def _cparams(sem, vmem=VMEM_LIMIT):
    return pltpu.CompilerParams(dimension_semantics=sem, vmem_limit_bytes=vmem)


def _rms(xf, g, n):
    ss = jnp.sum(xf * xf, axis=-1, keepdims=True)
    return xf * lax.rsqrt(ss * (1.0 / n) + EPS) * g


def _in_proj_kernel(x_ref, meta_ref, g_ref, w_ref, u_ref, lat_ref, mu_ref, mlat_ref, *, n_u, d):
    def proj(xf):
        xn = _rms(xf, g_ref[...], d).astype(BF16)
        return jnp.dot(xn, w_ref[...], preferred_element_type=F32)

    p = proj(x_ref[...])
    u_ref[...] = p[:, :n_u]
    lat_ref[...] = p[:, n_u:]

    @pl.when(pl.program_id(0) == 0)
    def _():
        pm = proj(meta_ref[...])
        mu_ref[...] = pm[:, :n_u]
        mlat_ref[...] = pm[:, n_u:]


def _in_proj(x2d, meta, g, w_p, n_u, tm):
    n, d = x2d.shape
    n_out = w_p.shape[1]
    n_lat = n_out - n_u
    return pl.pallas_call(
        functools.partial(_in_proj_kernel, n_u=n_u, d=d),
        grid=(n // tm,),
        in_specs=[
            pl.BlockSpec((tm, d), lambda i: (i, 0)),
            pl.BlockSpec((N_META, d), lambda i: (0, 0)),
            pl.BlockSpec((1, d), lambda i: (0, 0)),
            pl.BlockSpec((d, n_out), lambda i: (0, 0), pipeline_mode=pl.Buffered(1)),
        ],
        out_specs=[
            pl.BlockSpec((tm, n_u), lambda i: (i, 0)),
            pl.BlockSpec((tm, n_lat), lambda i: (i, 0)),
            pl.BlockSpec((N_META, n_u), lambda i: (0, 0)),
            pl.BlockSpec((N_META, n_lat), lambda i: (0, 0)),
        ],
        out_shape=[
            jax.ShapeDtypeStruct((n, n_u), F32),
            jax.ShapeDtypeStruct((n, n_lat), F32),
            jax.ShapeDtypeStruct((N_META, n_u), F32),
            jax.ShapeDtypeStruct((N_META, n_lat), F32),
        ],
        compiler_params=_cparams(("arbitrary",)),
        name="in_proj",
    )(x2d, meta, g, w_p)


def _lanes(a, m):
    return jnp.concatenate([a] * (m // 128), axis=1) if m > 128 else a


def _ones_rows(r, m):
    return jnp.where(lax.broadcasted_iota(jnp.int32, (r, m), 0) == 0, 1.0, 0.0).astype(BF16)


def _kv_rows(lat, tab, gkvl_ref, gkh_ref, wk_ref, wvt_ref, store_k, store_vt, *, n_heads, q_rank):
    kvl = lat[:, q_rank:q_rank + KV_LORA]
    kd = lat[:, q_rank + KV_LORA:q_rank + KV_LORA + 128]
    kr = lat[:, q_rank + KV_LORA + 128:q_rank + KV_LORA + 256]
    kvn = _rms(kvl, gkvl_ref[...], KV_LORA)
    kvn_b = kvn.astype(BF16)
    kvn_t = kvn.T.astype(BF16)
    gk0 = gkh_ref[:, :128]
    k_rope = kd * gkh_ref[:, 128:256] * tab[:, :128] + kr * gkh_ref[:, 256:384] * tab[:, 128:256]
    ss_kr = 0.5 * jnp.sum(kd * kd, axis=-1, keepdims=True)
    inv_n = 1.0 / QK_HEAD
    for p in range(n_heads // 2):
        kk = jnp.dot(kvn_b, wk_ref[:, p * 256:(p + 1) * 256], preferred_element_type=F32)
        vv = jnp.dot(wvt_ref[p * 256:(p + 1) * 256, :], kvn_t, preferred_element_type=F32)
        for t in range(2):
            kn = kk[:, t * 128:(t + 1) * 128]
            ssk = jnp.sum(kn * kn, axis=-1, keepdims=True) + ss_kr
            rk = lax.rsqrt(ssk * inv_n + EPS)
            store_k(2 * p + t, jnp.concatenate([kn * rk * gk0, k_rope * rk], axis=-1).astype(BF16))
            store_vt(2 * p + t, vv[t * 128:(t + 1) * 128].astype(BF16))


def _qkv_kernel(lat_ref, tab_ref, tqt_ref, mlat_ref, mtab_ref, gql_ref, gkvl_ref, gqc_ref, gkh_ref,
                wqt_ref, wk_ref, wvt_ref, qt_ref, k_ref, vt_ref, mk_ref, mvt_ref, *, n_heads, q_rank):
    lat = lat_ref[...]
    m = lat.shape[0]

    def sk(h, val):
        k_ref[h] = val

    def svt(h, val):
        vt_ref[h, :V_HEAD, :] = val
        vt_ref[h, V_HEAD:, :] = _ones_rows(V_AUG - V_HEAD, val.shape[1])

    _kv_rows(lat, tab_ref[...], gkvl_ref, gkh_ref, wk_ref, wvt_ref, sk, svt, n_heads=n_heads, q_rank=q_rank)

    scale = QK_HEAD ** -0.5 * LOG2E
    qn_t = _rms(lat[:, :q_rank], gql_ref[...], q_rank).T.astype(BF16)
    gq0 = _lanes(gqc_ref[:128, :], m) * scale
    gq1 = _lanes(gqc_ref[128:, :], m) * tqt_ref[...] * scale
    inv_n = 1.0 / QK_HEAD
    for h in range(n_heads):
        xh = jnp.dot(wqt_ref[h * HEAD_PAD:(h + 1) * HEAD_PAD, :], qn_t, preferred_element_type=F32)
        x0 = xh[:128]
        x1 = xh[128:]
        ss = jnp.sum(x0 * x0, axis=0, keepdims=True) + 0.5 * jnp.sum(x1 * x1, axis=0, keepdims=True)
        rinv = lax.rsqrt(ss * inv_n + EPS)
        qt_ref[h] = jnp.concatenate([x0 * rinv * gq0, x1 * rinv * gq1], axis=0).astype(BF16)

    @pl.when((pl.program_id(0) == 0) & (pl.program_id(1) == 0))
    def _():
        def smk(h, val):
            mk_ref[h] = val

        def smvt(h, val):
            mvt_ref[h, :V_HEAD, :] = val
            mvt_ref[h, V_HEAD:, :] = _ones_rows(V_AUG - V_HEAD, val.shape[1])

        _kv_rows(mlat_ref[...], mtab_ref[...], gkvl_ref, gkh_ref, wk_ref, wvt_ref, smk, smvt,
                 n_heads=n_heads, q_rank=q_rank)


def _qkv(lat, tab, tqt, mlat, mtab, gql, gkvl, gqc, gkh, wqt, wk, wvt, batch, seq, n_heads, q_rank, tm):
    n_lat = lat.shape[1]
    tpb = seq // tm
    mp = mlat.shape[0]
    const = lambda b, i: (0, 0)
    return pl.pallas_call(
        functools.partial(_qkv_kernel, n_heads=n_heads, q_rank=q_rank),
        grid=(batch, tpb),
        in_specs=[
            pl.BlockSpec((tm, n_lat), lambda b, i: (b * tpb + i, 0)),
            pl.BlockSpec((tm, 256), lambda b, i: (i, 0)),
            pl.BlockSpec((128, tm), lambda b, i: (0, i)),
            pl.BlockSpec((mp, n_lat), const),
            pl.BlockSpec((mp, 256), const),
            pl.BlockSpec((1, q_rank), const),
            pl.BlockSpec((1, KV_LORA), const),
            pl.BlockSpec((256, 128), const),
            pl.BlockSpec((1, 384), const),
            pl.BlockSpec(wqt.shape, const, pipeline_mode=pl.Buffered(1)),
            pl.BlockSpec(wk.shape, const, pipeline_mode=pl.Buffered(1)),
            pl.BlockSpec(wvt.shape, const, pipeline_mode=pl.Buffered(1)),
        ],
        out_specs=[
            pl.BlockSpec((None, n_heads, HEAD_PAD, tm), lambda b, i: (b, 0, 0, i)),
            pl.BlockSpec((None, n_heads, tm, HEAD_PAD), lambda b, i: (b, 0, i, 0)),
            pl.BlockSpec((None, n_heads, None, V_AUG, tm), lambda b, i: (b, 0, i, 0, 0)),
            pl.BlockSpec((n_heads, mp, HEAD_PAD), lambda b, i: (0, 0, 0)),
            pl.BlockSpec((n_heads, V_AUG, mp), lambda b, i: (0, 0, 0)),
        ],
        out_shape=[
            jax.ShapeDtypeStruct((batch, n_heads, HEAD_PAD, seq), BF16),
            jax.ShapeDtypeStruct((batch, n_heads, seq, HEAD_PAD), BF16),
            jax.ShapeDtypeStruct((batch, n_heads, tpb, V_AUG, tm), BF16),
            jax.ShapeDtypeStruct((n_heads, mp, HEAD_PAD), BF16),
            jax.ShapeDtypeStruct((n_heads, V_AUG, mp), BF16),
        ],
        compiler_params=_cparams(("arbitrary", "arbitrary")),
        name="qkv",
    )(lat, tab, tqt, mlat, mtab, gql, gkvl, gqc, gkh, wqt, wk, wvt)


ATTN_HEADS = 2
ATTN_STRIP = 256
ATTN_TQ = 2048
ATTN_TK = 1024
ATTN_SM_LAG = 1
ATTN_PV_LAG = 3


def _attn_kernel(qt_ref, k_ref, vt_ref, mk_ref, mvt_ref, o_ref, acc_ref, *, tq, tv):
    qi = pl.program_id(2)
    nh = qt_ref.shape[0]
    sw = ATTN_STRIP
    chains = [(hh, x) for x in range(tq // sw) for hh in range(nh)]
    qts = [qt_ref[hh, :, x * sw:(x + 1) * sw] for hh, x in chains]

    carry = []
    for ci, (hh, x) in enumerate(chains):
        s0 = jnp.dot(mk_ref[hh], qts[ci], preferred_element_type=F32)
        s0 = jnp.where(lax.broadcasted_iota(jnp.int32, s0.shape, 0) < N_META, s0, NEG_BIG)
        m0 = jnp.max(s0, axis=0, keepdims=True)
        carry.append(m0)
        acc_ref[hh, :, x * sw:(x + 1) * sw] = jnp.dot(mvt_ref[hh], jnp.exp2(s0 - m0).astype(BF16),
                                                     preferred_element_type=F32)

    tk = ATTN_TK
    kpq = tq // tk

    def n_keys(x, diag):
        if diag is None:
            return tk
        return max(0, min(tk, (x + 1) * sw - diag * tk))

    def scores(ci, j, diag):
        hh, x = chains[ci]
        start = pl.multiple_of(j * tk, tk)
        s = jnp.dot(k_ref[hh, pl.ds(start, n_keys(x, diag)), :], qts[ci],
                    preferred_element_type=F32)
        if diag is not None and (diag + 1) * tk - 1 > x * sw:
            key = lax.broadcasted_iota(jnp.int32, s.shape, 0) + diag * tk
            qry = lax.broadcasted_iota(jnp.int32, s.shape, 1) + x * sw
            s = jnp.where(key <= qry, s, NEG_BIG)
        return s

    def softmax(s, m_prev):
        m_new = jnp.maximum(m_prev, jnp.max(s, axis=0, keepdims=True))
        return m_new, jnp.exp2(m_prev - m_new), jnp.exp2((s - m_new).astype(BF16))

    def values(ci, j, alpha, pb):
        hh, x = chains[ci]
        pv = jnp.dot(vt_ref[hh, j * (tk // tv)], pb[:tv], preferred_element_type=F32)
        for c in range(1, pb.shape[0] // tv):
            pv += jnp.dot(vt_ref[hh, j * (tk // tv) + c], pb[c * tv:(c + 1) * tv],
                          preferred_element_type=F32)
        acc_ref[hh, :, x * sw:(x + 1) * sw] = alpha * acc_ref[hh, :, x * sw:(x + 1) * sw] + pv

    def step(j, c, diag=None):
        live = [ci for ci, (hh, x) in enumerate(chains) if n_keys(x, diag) > 0]
        n = len(live)
        out = list(c)
        s_vals = {}
        sm_vals = {}
        lag = ATTN_PV_LAG
        for t in range(n + lag):
            if t < n:
                s_vals[t] = scores(live[t], j, diag)
            if 0 <= t - ATTN_SM_LAG < n:
                ts = t - ATTN_SM_LAG
                ci = live[ts]
                out[ci], alpha, pb = softmax(s_vals.pop(ts), c[ci])
                sm_vals[ts] = (alpha, pb)
            if 0 <= t - lag < n:
                values(live[t - lag], j, *sm_vals.pop(t - lag))
        return tuple(out)

    c = lax.fori_loop(0, qi * kpq, lambda j, c: step(j, c), tuple(carry))
    for diag in range(kpq):
        c = step(qi * kpq + diag, c, diag)
    for hh, x in chains:
        acc = acc_ref[hh, :, x * sw:(x + 1) * sw]
        o_ref[x * sw:(x + 1) * sw, hh * V_HEAD:(hh + 1) * V_HEAD] = (
            acc[:V_HEAD] / acc[V_HEAD:V_HEAD + 1]).T.astype(o_ref.dtype)


def _attention(qt, k, vt, mk, mvt, tq):
    batch, n_heads, seq, _ = k.shape
    n_chunks, tv = vt.shape[2], vt.shape[4]
    mp = mk.shape[1]
    nh = ATTN_HEADS
    return pl.pallas_call(
        functools.partial(_attn_kernel, tq=tq, tv=tv),
        grid=(batch, n_heads // nh, seq // tq),
        in_specs=[
            pl.BlockSpec((None, nh, HEAD_PAD, tq), lambda b, h, i: (b, h, 0, i)),
            pl.BlockSpec((None, nh, seq, HEAD_PAD), lambda b, h, i: (b, h, 0, 0)),
            pl.BlockSpec((None, nh, n_chunks, V_AUG, tv), lambda b, h, i: (b, h, 0, 0, 0)),
            pl.BlockSpec((nh, mp, HEAD_PAD), lambda b, h, i: (h, 0, 0)),
            pl.BlockSpec((nh, V_AUG, mp), lambda b, h, i: (h, 0, 0)),
        ],
        out_specs=pl.BlockSpec((None, tq, nh * V_HEAD), lambda b, h, i: (b, i, h)),
        out_shape=jax.ShapeDtypeStruct((batch, seq, n_heads * V_HEAD), BF16),
        scratch_shapes=[pltpu.VMEM((nh, V_AUG, tq), F32)],
        compiler_params=_cparams(("arbitrary", "arbitrary", "arbitrary")),
        name="attention",
    )(qt, k, vt, mk, mvt)


def _out_proj_kernel(u_ref, halo_ref, mu_ref, y_ref, x_ref, wp_ref, ps_ref, wo_ref, o_ref,
                     *, tiles_per_batch, group):
    i = pl.program_id(1)
    first = (i % tiles_per_batch) == 0
    halo = jnp.where(first, mu_ref[...], halo_ref[...])
    ucat = jnp.concatenate([halo, u_ref[...]], axis=0)
    ys = []
    for gi, w in enumerate(POOL_WINDOWS):
        ug = ucat[:, gi * group:(gi + 1) * group]
        s = ug
        shift = 1
        while shift < w:
            s = s + pltpu.roll(s, shift, 0)
            shift *= 2
        diff = (s * (1.0 / w) - ug)[N_META:]
        yg = jnp.dot(diff.astype(BF16), wp_ref[gi], preferred_element_type=F32)
        ys.append((yg * ps_ref[:, gi * group:(gi + 1) * group]).astype(BF16))
    ycat = jnp.concatenate(ys + [y_ref[...]], axis=-1)
    o_ref[...] = x_ref[...] + jnp.dot(ycat, wo_ref[...], preferred_element_type=F32)


def _out_proj(u, mu, y_mla, x2d, w_pool, pool_scale, w_out, seq, tm, tn):
    n, d = x2d.shape
    pool_w = u.shape[1]
    mla_w = y_mla.shape[1]
    group = pool_w // len(POOL_WINDOWS)
    hb = tm // N_META
    return pl.pallas_call(
        functools.partial(_out_proj_kernel, tiles_per_batch=seq // tm, group=group),
        grid=(d // tn, n // tm),
        in_specs=[
            pl.BlockSpec((tm, pool_w), lambda j, i: (i, 0)),
            pl.BlockSpec((N_META, pool_w), lambda j, i: (jnp.maximum(i * hb - 1, 0), 0)),
            pl.BlockSpec((N_META, pool_w), lambda j, i: (0, 0)),
            pl.BlockSpec((tm, mla_w), lambda j, i: (i, 0)),
            pl.BlockSpec((tm, tn), lambda j, i: (i, j)),
            pl.BlockSpec(w_pool.shape, lambda j, i: (0, 0, 0)),
            pl.BlockSpec((1, pool_w), lambda j, i: (0, 0)),
            pl.BlockSpec((d, tn), lambda j, i: (0, j)),
        ],
        out_specs=pl.BlockSpec((tm, tn), lambda j, i: (i, j)),
        out_shape=jax.ShapeDtypeStruct((n, d), F32),
        compiler_params=_cparams(("arbitrary", "arbitrary")),
        name="out_proj",
    )(u, u, mu, y_mla, x2d, w_pool, pool_scale, w_out)


def _pack_bf16_pair(lo, hi):
    lo_b = lax.bitcast_convert_type(lo.astype(BF16).astype(F32), jnp.uint32)
    hi_b = lax.bitcast_convert_type(hi.astype(BF16).astype(F32), jnp.uint32)
    return lax.shift_right_logical(lo_b, jnp.uint32(16)) | hi_b


def _unpack_bf16_pair(w):
    lo = lax.bitcast_convert_type(lax.shift_left(w, jnp.uint32(16)), F32)
    hi = lax.bitcast_convert_type(w & jnp.uint32(0xFFFF0000), F32)
    return lo, hi


def _first_argmax(vals, sub, n):
    mx = jnp.max(vals, axis=0, keepdims=True)
    idx = jnp.min(jnp.where(vals == mx, sub, n), axis=0, keepdims=True)
    return mx, idx


def _router_kernel(h_ref, g_ref, wr_ref, br_ref, xp_ref, ri_ref, rf_ref, cnt_ref, *, d):
    xn = _rms(h_ref[...], g_ref[...], d)
    half = d // 2
    xp_ref[...] = _pack_bf16_pair(xn[:, :half], xn[:, half:])
    logits = lax.dot_general(wr_ref[...], xn.astype(BF16), (((1,), (1,)), ((), ())),
                             preferred_element_type=F32) + br_ref[...]
    tm = logits.shape[1]
    sub = lax.broadcasted_iota(jnp.int32, (N_GROUPS, tm), 0)
    lg = logits[:N_GROUPS]
    gmax, g_idx = _first_argmax(lg, sub, N_GROUPS)
    g_p = 1.0 / jnp.sum(jnp.exp(lg - gmax), axis=0, keepdims=True)
    in_group = jnp.zeros((EXPERTS_PER_GROUP, tm), F32)
    for g in range(N_GROUPS):
        lo = N_GROUPS + g * EXPERTS_PER_GROUP
        in_group = jnp.where(g_idx == g, logits[lo:lo + EXPERTS_PER_GROUP], in_group)
    m1, i1 = _first_argmax(in_group, sub, EXPERTS_PER_GROUP)
    rest = jnp.where(sub == i1, -jnp.inf, in_group)
    m2, i2 = _first_argmax(rest, sub, EXPERTS_PER_GROUP)
    e = jnp.exp(m2 - m1)
    w1 = g_p / (1.0 + e)
    w2 = g_p * e / (1.0 + e)
    e1 = g_idx * EXPERTS_PER_GROUP + i1
    e2 = g_idx * EXPERTS_PER_GROUP + i2
    ri_ref[...] = jnp.where(sub == 0, e1, jnp.where(sub == 1, e2, 0))
    rf_ref[:, :128] = jnp.broadcast_to(w1, (128, tm)).T
    rf_ref[:, 128:] = jnp.broadcast_to(w2, (128, tm)).T
    sub_e = lax.broadcasted_iota(jnp.int32, (N_EXPERTS, tm), 0)
    hit = jnp.where((sub_e == e1) | (sub_e == e2), 1.0, 0.0)
    cnt = jnp.sum(hit, axis=1, keepdims=True)

    @pl.when(pl.program_id(0) == 0)
    def _():
        cnt_ref[...] = jnp.zeros_like(cnt_ref)

    cnt_ref[...] += jnp.broadcast_to(cnt, cnt_ref.shape)


def _router(h1, g, wr_t, br, tm):
    n, d = h1.shape
    return pl.pallas_call(
        functools.partial(_router_kernel, d=d),
        grid=(n // tm,),
        in_specs=[
            pl.BlockSpec((tm, d), lambda i: (i, 0)),
            pl.BlockSpec((1, d), lambda i: (0, 0)),
            pl.BlockSpec((ROUTER_ROWS, d), lambda i: (0, 0)),
            pl.BlockSpec((ROUTER_ROWS, 1), lambda i: (0, 0)),
        ],
        out_specs=[
            pl.BlockSpec((tm, d // 2), lambda i: (i, 0)),
            pl.BlockSpec((8, tm), lambda i: (0, i)),
            pl.BlockSpec((tm, 256), lambda i: (i, 0)),
            pl.BlockSpec((N_EXPERTS, 128), lambda i: (0, 0)),
        ],
        out_shape=[
            jax.ShapeDtypeStruct((n, d // 2), jnp.uint32),
            jax.ShapeDtypeStruct((8, n), jnp.int32),
            jax.ShapeDtypeStruct((n, 256), F32),
            jax.ShapeDtypeStruct((N_EXPERTS, 128), F32),
        ],
        compiler_params=_cparams(("arbitrary",)),
        name="router",
    )(h1, g, wr_t, br)


def _dest_kernel(ri_ref, pstart_ref, tri_ref, d_ref, carry_ref):
    @pl.when(pl.program_id(0) == 0)
    def _():
        carry_ref[...] = jnp.zeros_like(carry_ref)

    tn = ri_ref.shape[1]
    e1 = ri_ref[0:1, :]
    e2 = ri_ref[1:2, :]
    sub = lax.broadcasted_iota(jnp.int32, (N_EXPERTS, tn), 0)
    oh1 = sub == e1
    oh2 = sub == e2
    oh = jnp.where(oh1 | oh2, 1.0, 0.0)
    c = jnp.dot(oh.astype(BF16), tri_ref[...], preferred_element_type=F32)
    val = c + (pstart_ref[...] + carry_ref[:, :1] - 1.0)
    d1 = jnp.sum(jnp.where(oh1, val, 0.0), axis=0, keepdims=True)
    d2 = jnp.sum(jnp.where(oh2, val, 0.0), axis=0, keepdims=True)
    sub8 = lax.broadcasted_iota(jnp.int32, (8, tn), 0)
    d_ref[...] = jnp.where(sub8 == 0, d1, jnp.where(sub8 == 1, d2, 0.0)).astype(jnp.int32)
    carry_ref[...] += jnp.broadcast_to(jnp.sum(oh, axis=1, keepdims=True), carry_ref.shape)


def _dest(ri, pstart_col, tn):
    n = ri.shape[1]
    tri = (jnp.arange(tn)[:, None] <= jnp.arange(tn)[None, :]).astype(BF16)
    return pl.pallas_call(
        _dest_kernel,
        grid=(n // tn,),
        in_specs=[
            pl.BlockSpec((8, tn), lambda i: (0, i)),
            pl.BlockSpec((N_EXPERTS, 1), lambda i: (0, 0)),
            pl.BlockSpec((tn, tn), lambda i: (0, 0)),
        ],
        out_specs=pl.BlockSpec((8, tn), lambda i: (0, i)),
        out_shape=jax.ShapeDtypeStruct((8, n), jnp.int32),
        scratch_shapes=[pltpu.VMEM((N_EXPERTS, 128), F32)],
        compiler_params=_cparams(("arbitrary",)),
        name="dest",
    )(ri, pstart_col, tri)


def _dispatch_kernel(d1_ref, d2_ref, cnt_ref, pstart_ref, padded_ref, npad_ref, nused_ref,
                     xp_ref, xs_ref, zero_ref, sem, zsem, bsem, *, tm, tb, n_blocks):
    i = pl.program_id(0)
    base = i * tm

    def row_copy(r, dst):
        return pltpu.make_async_copy(xp_ref.at[pl.ds(r, 1), :], xs_ref.at[pl.ds(dst, 1), :], sem)

    def zero_row_copy(dst):
        return pltpu.make_async_copy(zero_ref.at[pl.ds(0, 1), :], xs_ref.at[pl.ds(dst, 1), :], zsem)

    def zero_block_copy(blk):
        start = pl.multiple_of(blk * tb, tb)
        return pltpu.make_async_copy(zero_ref, xs_ref.at[pl.ds(start, tb), :], bsem)

    def issue(r, c):
        row_copy(r, d1_ref[base + r]).start(priority=0)
        row_copy(r, d2_ref[base + r]).start(priority=1)
        return c

    lax.fori_loop(0, tm, issue, 0, unroll=4)

    @pl.when(i == 0)
    def _():
        zero_ref[...] = jnp.zeros_like(zero_ref)

        def per_expert(e, c):
            lo = pstart_ref[e] + cnt_ref[e]
            hi = pstart_ref[e] + padded_ref[e]

            def per_row(p, c2):
                zero_row_copy(p).start()
                return c2

            lax.fori_loop(lo, hi, per_row, 0)
            return c

        lax.fori_loop(0, N_EXPERTS, per_expert, 0)

        def per_block(b, c):
            zero_block_copy(b).start()
            return c

        lax.fori_loop(nused_ref[0], n_blocks, per_block, 0)

        def zwait(p, c):
            zero_row_copy(0).wait()
            return c

        lax.fori_loop(0, npad_ref[0], zwait, 0)

        def bwait(b, c):
            zero_block_copy(0).wait()
            return c

        lax.fori_loop(nused_ref[0], n_blocks, bwait, 0)

    for _ in range(2):
        pltpu.make_async_copy(xp_ref, xs_ref.at[pl.ds(0, tm), :], sem).wait()


def _dispatch(d1, d2, cnt, pstart, padded, npad, n_used, xp, n_rows, tm, tb):
    n, half = xp.shape
    grid_spec = pltpu.PrefetchScalarGridSpec(
        num_scalar_prefetch=7,
        grid=(n // tm,),
        in_specs=[pl.BlockSpec((tm, half), lambda i, *_: (i, 0))],
        out_specs=pl.BlockSpec(memory_space=pl.ANY),
        scratch_shapes=[
            pltpu.VMEM((tb, half), jnp.uint32),
            pltpu.SemaphoreType.DMA,
            pltpu.SemaphoreType.DMA,
            pltpu.SemaphoreType.DMA,
        ],
    )
    return pl.pallas_call(
        functools.partial(_dispatch_kernel, tm=tm, tb=tb, n_blocks=n_rows // tb),
        grid_spec=grid_spec,
        out_shape=jax.ShapeDtypeStruct((n_rows, half), jnp.uint32),
        compiler_params=pltpu.CompilerParams(
            dimension_semantics=("arbitrary",), vmem_limit_bytes=VMEM_LIMIT, has_side_effects=True),
        name="dispatch",
    )(d1, d2, cnt, pstart, padded, npad, n_used, xp)


def _experts_kernel(be_ref, nused_ref, slot_ref, nexte_ref, xs_ref, wg_hbm, wu_hbm, wd_hbm, ys_ref,
                    wg_ref, wu_ref, wd_ref, gu_sem, wd_sem):
    i = pl.program_id(0)
    active = i < nused_ref[0]
    fresh = (i == 0) | (be_ref[i] != be_ref[jnp.maximum(i - 1, 0)])
    slot = slot_ref[i]

    def gu_copies(e, s):
        return (pltpu.make_async_copy(wg_hbm.at[e], wg_ref.at[s], gu_sem.at[s]),
                pltpu.make_async_copy(wu_hbm.at[e], wu_ref.at[s], gu_sem.at[s]))

    def wd_copy():
        return pltpu.make_async_copy(wd_hbm.at[be_ref[i]], wd_ref, wd_sem)

    @pl.when(active & (i == 0))
    def _():
        for cp in gu_copies(be_ref[0], 0):
            cp.start()

    @pl.when(active & fresh)
    def _():
        wd_copy().start()
        for cp in gu_copies(be_ref[i], slot):
            cp.wait()

        @pl.when(nexte_ref[i] >= 0)
        def _():
            for cp in gu_copies(nexte_ref[i], 1 - slot):
                cp.start()

    @pl.when(active)
    def _():
        lo, hi = _unpack_bf16_pair(xs_ref[...])
        half = lo.shape[1]
        g = (jnp.dot(lo, wg_ref[slot, :half], preferred_element_type=F32)
             + jnp.dot(hi, wg_ref[slot, half:], preferred_element_type=F32))
        u = (jnp.dot(lo, wu_ref[slot, :half], preferred_element_type=F32)
             + jnp.dot(hi, wu_ref[slot, half:], preferred_element_type=F32))
        hb = g / (1.0 + jnp.exp(-g)) * u

        @pl.when(fresh)
        def _():
            wd_copy().wait()

        y = jnp.dot(hb, wd_ref[...], preferred_element_type=F32)
        ys_ref[...] = _pack_bf16_pair(y[:, :half], y[:, half:])

    @pl.when(i >= nused_ref[0])
    def _():
        ys_ref[...] = jnp.zeros_like(ys_ref)


def _experts(blk_expert, n_used, xs, wg, wu, wd, tm):
    n_rows, half = xs.shape
    _, d, f = wg.shape
    n_blocks = n_rows // tm

    idx = jnp.arange(n_blocks, dtype=jnp.int32)
    fresh = jnp.concatenate([jnp.ones((1,), bool), blk_expert[1:] != blk_expert[:-1]]) & (idx < n_used[0])
    slot = ((jnp.cumsum(fresh.astype(jnp.int32)) - 1) % 2).astype(jnp.int32)
    later = jnp.where(fresh, idx, n_blocks)
    nxt = lax.cummin(jnp.concatenate([later[1:], jnp.full((1,), n_blocks, jnp.int32)]), reverse=True)
    next_e = jnp.where(nxt < n_blocks, blk_expert[jnp.minimum(nxt, n_blocks - 1)], -1).astype(jnp.int32)

    def row_map(i, *_):
        return (i, 0)

    grid_spec = pltpu.PrefetchScalarGridSpec(
        num_scalar_prefetch=4,
        grid=(n_blocks,),
        in_specs=[
            pl.BlockSpec((tm, half), row_map),
            pl.BlockSpec(memory_space=pl.ANY),
            pl.BlockSpec(memory_space=pl.ANY),
            pl.BlockSpec(memory_space=pl.ANY),
        ],
        out_specs=pl.BlockSpec((tm, half), row_map),
        scratch_shapes=[
            pltpu.VMEM((2, d, f), F32),
            pltpu.VMEM((2, d, f), F32),
            pltpu.VMEM((f, d), F32),
            pltpu.SemaphoreType.DMA((2,)),
            pltpu.SemaphoreType.DMA,
        ],
    )
    return pl.pallas_call(
        _experts_kernel,
        grid_spec=grid_spec,
        out_shape=jax.ShapeDtypeStruct((n_rows, half), jnp.uint32),
        compiler_params=_cparams(("arbitrary",)),
        name="experts",
    )(blk_expert, n_used, slot, next_e, xs, wg, wu, wd)


def _combine_kernel(d1_ref, d2_ref, h_ref, w_ref, ys_ref, o_ref, a_ref, b_ref, sem, *, tm, n_steps):
    i = pl.program_id(0)
    slot = i % 2

    def gather(step, s):
        base = step * tm

        def row_copy(src, r, buf):
            return pltpu.make_async_copy(ys_ref.at[pl.ds(src, 1), :], buf.at[s, pl.ds(r, 1), :], sem.at[s])

        def issue(r, c):
            row_copy(d1_ref[base + r], r, a_ref).start(priority=0)
            row_copy(d2_ref[base + r], r, b_ref).start(priority=1)
            return c

        lax.fori_loop(0, tm, issue, 0, unroll=4)

    @pl.when(i == 0)
    def _():
        gather(0, 0)

    @pl.when(i + 1 < n_steps)
    def _():
        gather(i + 1, 1 - slot)

    for buf in (a_ref, b_ref):
        pltpu.make_async_copy(ys_ref.at[pl.ds(0, tm), :], buf.at[slot], sem.at[slot]).wait()

    half = a_ref.shape[2]
    rows = 8

    def slab(r, c):
        r0 = pl.multiple_of(r * rows, rows)
        a_lo, a_hi = _unpack_bf16_pair(a_ref[slot, pl.ds(r0, rows), :])
        b_lo, b_hi = _unpack_bf16_pair(b_ref[slot, pl.ds(r0, rows), :])
        w1 = _lanes(w_ref[pl.ds(r0, rows), :128], half)
        w2 = _lanes(w_ref[pl.ds(r0, rows), 128:], half)
        o_ref[pl.ds(r0, rows), :half] = h_ref[pl.ds(r0, rows), :half] + (w1 * a_lo + w2 * b_lo)
        o_ref[pl.ds(r0, rows), half:] = h_ref[pl.ds(r0, rows), half:] + (w1 * a_hi + w2 * b_hi)
        return c

    lax.fori_loop(0, tm // rows, slab, 0, unroll=2)


def _combine(d1, d2, h1, wcol, ys, tm):
    n, d = h1.shape
    half = d // 2
    grid_spec = pltpu.PrefetchScalarGridSpec(
        num_scalar_prefetch=2,
        grid=(n // tm,),
        in_specs=[
            pl.BlockSpec((tm, d), lambda i, *_: (i, 0)),
            pl.BlockSpec((tm, 256), lambda i, *_: (i, 0)),
            pl.BlockSpec(memory_space=pl.ANY),
        ],
        out_specs=pl.BlockSpec((tm, d), lambda i, *_: (i, 0)),
        scratch_shapes=[
            pltpu.VMEM((2, tm, half), jnp.uint32),
            pltpu.VMEM((2, tm, half), jnp.uint32),
            pltpu.SemaphoreType.DMA((2,)),
        ],
    )
    return pl.pallas_call(
        functools.partial(_combine_kernel, tm=tm, n_steps=n // tm),
        grid_spec=grid_spec,
        out_shape=jax.ShapeDtypeStruct((n, d), F32),
        compiler_params=_cparams(("arbitrary",)),
        name="combine",
    )(d1, d2, h1, wcol, ys)


def _rope_tab(pos):
    inv = 1.0 / (ROPE_THETA ** (jnp.arange(0, QK_ROPE, 2, dtype=F32) / QK_ROPE))
    ang = pos.astype(F32)[:, None] * inv[None, :]
    c = jnp.cos(ang)
    s = jnp.sin(ang)
    cos64 = jnp.concatenate([c, c], axis=-1)
    sin64 = jnp.concatenate([-s, s], axis=-1)
    tab_q = jnp.concatenate([cos64, sin64], axis=-1)
    tab_k = jnp.concatenate([cos64, cos64, sin64, sin64], axis=-1)
    return tab_q, tab_k


def _swap_halves(a):
    h = a.shape[-1] // 2
    return jnp.concatenate([a[..., h:], a[..., :h]], axis=-1)


def _tile(n, pref):
    t = pref
    while n % t:
        t //= 2
    return t


def kernel(x, meta_tokens, mix_norm_g, w_in, q_lat_norm_g, w_uq, kv_lat_norm_g, w_ukv, q_head_norm_g,
           k_head_norm_g, w_pool, pool_scale, w_out, ffn_norm_g, w_group, b_group, w_expert, b_expert,
           w_gate, w_up, w_down):
    batch, seq, d = x.shape
    depth = w_in.shape[0]
    pool_w = pool_scale.shape[-1]
    q_rank = q_lat_norm_g.shape[-1]
    n_heads = w_uq.shape[-1] // QK_HEAD
    n = batch * seq
    f = w_gate.shape[-1]

    tab_q, tab_k = _rope_tab(N_META + jnp.arange(seq))
    tab_qt = tab_q.T
    meta_rows = 128
    _, mtab_k = _rope_tab(jnp.arange(meta_rows))
    tm_exp = _tile(n, 256)
    n_blocks = (2 * n + N_EXPERTS * (tm_exp - 1) + tm_exp - 1) // tm_exp
    n_rows = n_blocks * tm_exp

    h = x.reshape(n, d)
    meta = meta_tokens.astype(F32)
    for l in range(depth):
        k_r = w_in[l][:, pool_w + q_rank + KV_LORA:]
        k_rs = _swap_halves(k_r)
        w_in_p = jnp.concatenate([w_in[l][:, :pool_w + q_rank + KV_LORA], k_r, k_r, k_rs, k_rs],
                                 axis=-1).astype(BF16)
        wq = w_uq[l].reshape(q_rank, n_heads, QK_HEAD)
        wuq_p = jnp.concatenate([wq, _swap_halves(wq[..., QK_NOPE:])], axis=-1)
        wqt = wuq_p.reshape(q_rank, n_heads * HEAD_PAD).T.astype(BF16)
        wkv = w_ukv[l].reshape(KV_LORA, n_heads, QK_NOPE + V_HEAD)
        wk = wkv[..., :QK_NOPE].reshape(KV_LORA, n_heads * QK_NOPE).astype(BF16)
        wvt = wkv[..., QK_NOPE:].reshape(KV_LORA, n_heads * V_HEAD).T.astype(BF16)
        gq = q_head_norm_g[l]
        gk = k_head_norm_g[l]
        gqc = jnp.broadcast_to(jnp.concatenate([gq, _swap_halves(gq[QK_NOPE:])])[:, None], (HEAD_PAD, 128))
        gk_r = gk[QK_NOPE:]
        gkh = jnp.concatenate([gk[:QK_NOPE], gk_r, gk_r, _swap_halves(gk_r), _swap_halves(gk_r)])[None, :]
        wr_t = jnp.zeros((ROUTER_ROWS, d), F32)
        wr_t = wr_t.at[:N_GROUPS].set(w_group[l].T).at[N_GROUPS:N_GROUPS + N_EXPERTS].set(w_expert[l].T)
        br = jnp.zeros((ROUTER_ROWS, 1), F32)
        br = br.at[:N_GROUPS, 0].set(b_group[l]).at[N_GROUPS:N_GROUPS + N_EXPERTS, 0].set(b_expert[l])

        u, lat, mu, mlat = _in_proj(h, meta, mix_norm_g[l][None, :], w_in_p, pool_w, _tile(n, 256))
        mlat_p = jnp.pad(mlat, ((0, meta_rows - N_META), (0, 0)))
        qt, k, vt, mk, mvt = _qkv(lat, tab_k, tab_qt, mlat_p, mtab_k, q_lat_norm_g[l][None, :],
                                  kv_lat_norm_g[l][None, :], gqc, gkh, wqt, wk, wvt,
                                  batch, seq, n_heads, q_rank, _tile(seq, 256))
        y_mla = _attention(qt, k, vt, mk, mvt, _tile(seq, ATTN_TQ)).reshape(n, n_heads * V_HEAD)
        h1 = _out_proj(u, mu, y_mla, h, w_pool[l].astype(BF16), pool_scale[l][None, :],
                       w_out[l].astype(BF16), seq, _tile(seq, 256), _tile(d, 2048))

        xp, ri, rf, cnt = _router(h1, ffn_norm_g[l][None, :], wr_t.astype(BF16), br, _tile(n, 512))
        counts = cnt[:, 0].astype(jnp.int32)
        padded = (counts + tm_exp - 1) // tm_exp * tm_exp
        pend = jnp.cumsum(padded)
        pstart = pend - padded
        n_used = (pend[-1] // tm_exp).astype(jnp.int32)
        blk_start = jnp.arange(n_blocks, dtype=jnp.int32) * tm_exp
        blk_expert = jnp.sum((pend[None, :] <= blk_start[:, None]).astype(jnp.int32), axis=1)
        blk_expert = jnp.minimum(blk_expert, N_EXPERTS - 1)
        npad = (pend[-1] - 2 * n).astype(jnp.int32)

        dest = _dest(ri, pstart.astype(F32)[:, None], _tile(n, 512))
        d1 = dest[0]
        d2 = dest[1]
        xs = _dispatch(d1, d2, counts, pstart.astype(jnp.int32), padded.astype(jnp.int32),
                       npad[None], n_used[None], xp, n_rows, _tile(n, 1024), tm_exp)
        ys = _experts(blk_expert, n_used[None], xs, w_gate[l], w_up[l], w_down[l], tm_exp)
        h = _combine(d1, d2, h1, rf, ys, _tile(n, 256))
        if l + 1 < depth:
            raise NotImplementedError("meta-token outputs are only produced for a single layer")
    return h.reshape(batch, seq, d)
```
